```python
import jax, jax.numpy as jnp
from jax import lax
import numpy as np

D_MODEL = 1024
BATCH = 4
SEQ = 4096
DEPTH = 4
DEC_BATCH = 8
DEC_SEQ = 32
PAST_LEN = 1024

CHUNK = 64
N_MIXERS = 2
N_SGU_LAYERS = (DEPTH + 1) // 2
N_CONV_LAYERS = DEPTH // 2
SGU_CHUNK = 128
SGU_HEADS = 4
D_SGU_FFN = 6 * D_MODEL
D_SGU = D_SGU_FFN // 2
SGU_HEAD_DIM = D_SGU // SGU_HEADS
CONV_WIDTH = 31
CONV_STATE = CONV_WIDTH - 1
D_FF = -(-8 * D_MODEL // (3 * 256)) * 256
RMS_EPS = 1e-6
LN_EPS = 1e-5

kernel_name = "streaming_gmlp_conformer_conv_hybrid"


def rms_norm(x, g):
    xf = x.astype(jnp.float32)
    y = xf * lax.rsqrt(jnp.mean(xf * xf, axis=-1, keepdims=True) + RMS_EPS)
    return (y * g.astype(jnp.float32)).astype(x.dtype)


def layer_norm(x, g, b):
    xf = x.astype(jnp.float32)
    mu = jnp.mean(xf, axis=-1, keepdims=True)
    xc = xf - mu
    var = jnp.mean(xc * xc, axis=-1, keepdims=True)
    y = xc * lax.rsqrt(var + LN_EPS)
    return (y * g.astype(jnp.float32) + b.astype(jnp.float32)).astype(x.dtype)


def sgu_mixer(h, w_in, b_in, ln_g, ln_b, w_s, b_s, w_out, b_out):
    B, T, _ = h.shape
    z = jax.nn.gelu(h @ w_in + b_in)
    u, v = z[..., :D_SGU], z[..., D_SGU:]
    v = layer_norm(v, ln_g, ln_b)
    L = min(T, SGU_CHUNK)
    C = T // L
    mask = jnp.tril(jnp.ones((SGU_CHUNK, SGU_CHUNK), dtype=w_s.dtype))
    ws = (w_s * mask)[:, :L, :L]
    vc = v.reshape(B, C, L, SGU_HEADS, SGU_HEAD_DIM)
    mixed = jnp.einsum('hij,bcjhd->bcihd', ws, vc) + b_s[:, :L].T[None, None, :, :, None]
    gated = u * mixed.reshape(B, T, D_SGU)
    y = gated @ w_out + b_out
    return y, v


def conv_mixer(h, past, w_pw1, b_pw1, w_dw, b_dw, ln_g, ln_b, w_pw2, b_pw2):
    a = h @ w_pw1 + b_pw1
    glu = a[..., :D_MODEL] * jax.nn.sigmoid(a[..., D_MODEL:])
    padded = jnp.concatenate([past, glu], axis=1)
    new_state = padded[:, -CONV_STATE:]
    c = lax.conv_general_dilated(
        padded, w_dw[:, None, :], window_strides=(1,), padding='VALID',
        dimension_numbers=('NWC', 'WIO', 'NWC'), feature_group_count=D_MODEL) + b_dw
    c = jax.nn.silu(layer_norm(c, ln_g, ln_b))
    y = c @ w_pw2 + b_pw2
    return y, new_state


def swiglu(h, w_gate, w_up, w_down):
    return (jax.nn.silu(h @ w_gate) * (h @ w_up)) @ w_down


def trunk(x, conv_past, norm_mix_g, norm_ffn_g, norm_final_g,
          sgu_w_in, sgu_b_in, sgu_ln_g, sgu_ln_b, sgu_w_s, sgu_b_s, sgu_w_out, sgu_b_out,
          conv_w_pw1, conv_b_pw1, conv_w_dw, conv_b_dw, conv_ln_g, conv_ln_b, conv_w_pw2, conv_b_pw2,
          ffn_w_gate, ffn_w_up, ffn_w_down):
    conv_states, sgu_vs = [], []
    for i in range(DEPTH):
        h = rms_norm(x, norm_mix_g[i])
        j = i // N_MIXERS
        if i % N_MIXERS == 0:
            y, v = sgu_mixer(h, sgu_w_in[j], sgu_b_in[j], sgu_ln_g[j], sgu_ln_b[j],
                             sgu_w_s[j], sgu_b_s[j], sgu_w_out[j], sgu_b_out[j])
            sgu_vs.append(v)
        else:
            y, st = conv_mixer(h, conv_past[j], conv_w_pw1[j], conv_b_pw1[j], conv_w_dw[j],
                               conv_b_dw[j], conv_ln_g[j], conv_ln_b[j], conv_w_pw2[j], conv_b_pw2[j])
            conv_states.append(st)
        x = x + y
        h = rms_norm(x, norm_ffn_g[i])
        x = x + swiglu(h, ffn_w_gate[i], ffn_w_up[i], ffn_w_down[i])
    return rms_norm(x, norm_final_g), jnp.stack(conv_states), jnp.stack(sgu_vs)


def setup_inputs(seed: int = 0) -> dict:
    key = jax.random.key(seed)
    ks = jax.random.split(key, 32)
    f32 = jnp.float32

    def nrm(k, shape, scale):
        return jax.random.normal(k, shape, f32) * scale

    def gain(k, shape):
        return 1.0 + 0.02 * jax.random.normal(k, shape, f32)

    return {
        "x_prompt": nrm(ks[0], (BATCH, SEQ, D_MODEL), 1.0),
        "x_sample": nrm(ks[1], (DEC_BATCH, DEC_SEQ, D_MODEL), 1.0),
        "state_conv": nrm(ks[2], (N_CONV_LAYERS, DEC_BATCH, CONV_STATE, D_MODEL), 0.5),
        "norm_mix_g": gain(ks[3], (DEPTH, D_MODEL)),
        "norm_ffn_g": gain(ks[4], (DEPTH, D_MODEL)),
        "norm_final_g": gain(ks[5], (D_MODEL,)),
        "sgu_w_in": nrm(ks[6], (N_SGU_LAYERS, D_MODEL, D_SGU_FFN), D_MODEL ** -0.5),
        "sgu_b_in": nrm(ks[7], (N_SGU_LAYERS, D_SGU_FFN), 0.02),
        "sgu_ln_g": gain(ks[8], (N_SGU_LAYERS, D_SGU)),
        "sgu_ln_b": nrm(ks[9], (N_SGU_LAYERS, D_SGU), 0.02),
        "sgu_w_s": nrm(ks[10], (N_SGU_LAYERS, SGU_HEADS, SGU_CHUNK, SGU_CHUNK), 0.5 * SGU_CHUNK ** -0.5),
        "sgu_b_s": gain(ks[11], (N_SGU_LAYERS, SGU_HEADS, SGU_CHUNK)),
        "sgu_w_out": nrm(ks[12], (N_SGU_LAYERS, D_SGU, D_MODEL), D_SGU ** -0.5),
        "sgu_b_out": nrm(ks[13], (N_SGU_LAYERS, D_MODEL), 0.02),
        "conv_w_pw1": nrm(ks[14], (N_CONV_LAYERS, D_MODEL, 2 * D_MODEL), D_MODEL ** -0.5),
        "conv_b_pw1": nrm(ks[15], (N_CONV_LAYERS, 2 * D_MODEL), 0.02),
        "conv_w_dw": nrm(ks[16], (N_CONV_LAYERS, CONV_WIDTH, D_MODEL), CONV_WIDTH ** -0.5),
        "conv_b_dw": nrm(ks[17], (N_CONV_LAYERS, D_MODEL), 0.02),
        "conv_ln_g": gain(ks[18], (N_CONV_LAYERS, D_MODEL)),
        "conv_ln_b": nrm(ks[19], (N_CONV_LAYERS, D_MODEL), 0.02),
        "conv_w_pw2": nrm(ks[20], (N_CONV_LAYERS, D_MODEL, D_MODEL), D_MODEL ** -0.5),
        "conv_b_pw2": nrm(ks[21], (N_CONV_LAYERS, D_MODEL), 0.02),
        "ffn_w_gate": nrm(ks[22], (DEPTH, D_MODEL, D_FF), D_MODEL ** -0.5),
        "ffn_w_up": nrm(ks[23], (DEPTH, D_MODEL, D_FF), D_MODEL ** -0.5),
        "ffn_w_down": nrm(ks[24], (DEPTH, D_FF, D_MODEL), D_FF ** -0.5),
    }


def reference(x_prompt, x_sample, state_conv, norm_mix_g, norm_ffn_g, norm_final_g,
              sgu_w_in, sgu_b_in, sgu_ln_g, sgu_ln_b, sgu_w_s, sgu_b_s, sgu_w_out, sgu_b_out,
              conv_w_pw1, conv_b_pw1, conv_w_dw, conv_b_dw, conv_ln_g, conv_ln_b, conv_w_pw2, conv_b_pw2,
              ffn_w_gate, ffn_w_up, ffn_w_down):
    weights = (norm_mix_g, norm_ffn_g, norm_final_g,
               sgu_w_in, sgu_b_in, sgu_ln_g, sgu_ln_b, sgu_w_s, sgu_b_s, sgu_w_out, sgu_b_out,
               conv_w_pw1, conv_b_pw1, conv_w_dw, conv_b_dw, conv_ln_g, conv_ln_b, conv_w_pw2, conv_b_pw2,
               ffn_w_gate, ffn_w_up, ffn_w_down)
    zero_past = jnp.zeros((N_CONV_LAYERS, x_prompt.shape[0], CONV_STATE, D_MODEL), dtype=x_prompt.dtype)
    y_prompt, new_conv_prompt, _ = trunk(x_prompt, zero_past, *weights)
    y_sample, new_conv_sample, new_sgu_v_sample = trunk(x_sample, state_conv, *weights)
    return (y_prompt, y_sample, new_conv_prompt, new_conv_sample, new_sgu_v_sample)
```

```python
import functools

import jax
import jax.numpy as jnp
from jax import lax
from jax.experimental import pallas as pl
from jax.experimental.pallas import tpu as pltpu

D_MODEL = 1024
DEPTH = 4
SGU_CHUNK = 128
SGU_HEADS = 4
D_SGU = 3 * D_MODEL
SGU_HEAD_DIM = D_SGU // SGU_HEADS
CONV_WIDTH = 31
CONV_STATE = CONV_WIDTH - 1
D_FF = 2816
RMS_EPS = 1e-6
LN_EPS = 1e-5

LANES = 128
HALO = 32
HALO_PAD = HALO - CONV_STATE
COL_CHUNK = 256
VMEM_LIMIT_BYTES = 56 * 1024 * 1024

PROMPT_TILE_SGU = 256
PROMPT_TILE_CONV = 512
PROMPT_TILE_FFN = 512

F32 = jnp.float32
BF16 = jnp.bfloat16


def _rms(x, g):
    return x * lax.rsqrt(jnp.mean(x * x, axis=-1, keepdims=True) + RMS_EPS) * g


def _dot(a, b):
    return jnp.dot(a, b, preferred_element_type=F32)


def _lane_partial(v):
    acc = v[:, :LANES]
    for k in range(LANES, v.shape[1], LANES):
        acc = acc + v[:, k:k + LANES]
    return acc


def _ffn_kernel(*refs, final):
    if final:
        x_ref, g_ref, wg_ref, wu_ref, wd_ref, gf_ref, o_ref, act_ref = refs
    else:
        x_ref, g_ref, wg_ref, wu_ref, wd_ref, o_ref, act_ref = refs
    x = x_ref[...]
    h = _rms(x, g_ref[...]).astype(BF16)
    for j in range(0, D_FF, COL_CHUNK):
        g = _dot(h, wg_ref[:, j:j + COL_CHUNK])
        u = _dot(h, wu_ref[:, j:j + COL_CHUNK])
        act_ref[:, j:j + COL_CHUNK] = (jax.nn.silu(g) * u).astype(BF16)
    xo = x + _dot(act_ref[...], wd_ref[...])
    if final:
        xo = _rms(xo, gf_ref[...])
    o_ref[...] = xo


def _const_spec(shape):
    nd = len(shape)
    return pl.BlockSpec(shape, lambda i: (0,) * nd, pipeline_mode=pl.Buffered(1))


def _params():
    return pltpu.CompilerParams(dimension_semantics=("arbitrary",),
                                vmem_limit_bytes=VMEM_LIMIT_BYTES)


def _ffn(x, g, wg, wu, wd, gf, tile):
    rows = x.shape[0]
    final = gf is not None
    row_spec = pl.BlockSpec((tile, D_MODEL), lambda i: (i, 0))
    in_specs = [row_spec, _const_spec((1, D_MODEL)), _const_spec((D_MODEL, D_FF)),
                _const_spec((D_MODEL, D_FF)), _const_spec((D_FF, D_MODEL))]
    args = [x, g, wg, wu, wd]
    if final:
        in_specs.append(_const_spec((1, D_MODEL)))
        args.append(gf)
    return pl.pallas_call(
        functools.partial(_ffn_kernel, final=final),
        out_shape=jax.ShapeDtypeStruct((rows, D_MODEL), F32),
        grid=(rows // tile,),
        in_specs=in_specs,
        out_specs=row_spec,
        scratch_shapes=[pltpu.VMEM((tile, D_FF), BF16)],
        compiler_params=_params(),
        name="ffn_final" if final else "ffn",
    )(*args)


def _sgu_kernel(*refs, chunk_len, emit_v):
    if emit_v:
        (x_ref, g_ref, win_ref, bin_ref, lng_ref, lnb_ref, ws_ref, bsb_ref, wout_ref, bout_ref,
         o_ref, v_ref, u_scr, v_scr, vn_scr, gated_scr) = refs
    else:
        (x_ref, g_ref, win_ref, bin_ref, lng_ref, lnb_ref, ws_ref, bsb_ref, wout_ref, bout_ref,
         o_ref, u_scr, v_scr, vn_scr, gated_scr) = refs
    tile = x_ref.shape[0]
    x = x_ref[...]
    h = _rms(x, g_ref[...]).astype(BF16)

    for j in range(0, D_SGU, COL_CHUNK):
        z = _dot(h, win_ref[:, j:j + COL_CHUNK]) + bin_ref[:, j:j + COL_CHUNK]
        u_scr[:, j:j + COL_CHUNK] = jax.nn.gelu(z, approximate=True)
    s1 = jnp.zeros((tile, LANES), F32)
    s2 = jnp.zeros((tile, LANES), F32)
    for j in range(0, D_SGU, COL_CHUNK):
        z = _dot(h, win_ref[:, D_SGU + j:D_SGU + j + COL_CHUNK]) + bin_ref[:, D_SGU + j:D_SGU + j + COL_CHUNK]
        v = jax.nn.gelu(z, approximate=True)
        s1 = s1 + _lane_partial(v)
        s2 = s2 + _lane_partial(v * v)
        v_scr[:, j:j + COL_CHUNK] = v
    mean = jnp.sum(s1, axis=-1, keepdims=True) * (1.0 / D_SGU)
    var = jnp.sum(s2, axis=-1, keepdims=True) * (1.0 / D_SGU) - mean * mean
    rstd = lax.rsqrt(var + LN_EPS)
    for j in range(0, D_SGU, COL_CHUNK):
        vn = (v_scr[:, j:j + COL_CHUNK] - mean) * rstd * lng_ref[:, j:j + COL_CHUNK] + lnb_ref[:, j:j + COL_CHUNK]
        if emit_v:
            v_ref[:, j:j + COL_CHUNK] = vn
        vn_scr[:, j:j + COL_CHUNK] = vn.astype(BF16)

    row = lax.broadcasted_iota(jnp.int32, (chunk_len, chunk_len), 0)
    col = lax.broadcasted_iota(jnp.int32, (chunk_len, chunk_len), 1)
    for hd in range(SGU_HEADS):
        w = jnp.where(row >= col, ws_ref[hd], 0.0).astype(BF16)
        bias = jnp.concatenate([bsb_ref[hd]] * (SGU_HEAD_DIM // LANES), axis=1)
        c0 = hd * SGU_HEAD_DIM
        for r0 in range(0, tile, chunk_len):
            mixed = _dot(w, vn_scr[r0:r0 + chunk_len, c0:c0 + SGU_HEAD_DIM]) + bias
            gated_scr[r0:r0 + chunk_len, c0:c0 + SGU_HEAD_DIM] = (
                u_scr[r0:r0 + chunk_len, c0:c0 + SGU_HEAD_DIM] * mixed).astype(BF16)

    o_ref[...] = x + _dot(gated_scr[...], wout_ref[...]) + bout_ref[...]


def _sgu(x, g, w_in, b_in, ln_g, ln_b, ws, bsb, w_out, b_out, tile, chunk_len, emit_v):
    rows = x.shape[0]
    row_spec = pl.BlockSpec((tile, D_MODEL), lambda i: (i, 0))
    in_specs = [row_spec, _const_spec((1, D_MODEL)), _const_spec((D_MODEL, 2 * D_SGU)),
                _const_spec((1, 2 * D_SGU)), _const_spec((1, D_SGU)), _const_spec((1, D_SGU)),
                _const_spec((SGU_HEADS, chunk_len, chunk_len)), _const_spec((SGU_HEADS, chunk_len, LANES)),
                _const_spec((D_SGU, D_MODEL)), _const_spec((1, D_MODEL))]
    out_shape = jax.ShapeDtypeStruct((rows, D_MODEL), F32)
    out_specs = row_spec
    if emit_v:
        out_shape = (out_shape, jax.ShapeDtypeStruct((rows, D_SGU), F32))
        out_specs = (row_spec, pl.BlockSpec((tile, D_SGU), lambda i: (i, 0)))
    return pl.pallas_call(
        functools.partial(_sgu_kernel, chunk_len=chunk_len, emit_v=emit_v),
        out_shape=out_shape,
        grid=(rows // tile,),
        in_specs=in_specs,
        out_specs=out_specs,
        scratch_shapes=[pltpu.VMEM((tile, D_SGU), F32), pltpu.VMEM((tile, D_SGU), F32),
                        pltpu.VMEM((tile, D_SGU), BF16), pltpu.VMEM((tile, D_SGU), BF16)],
        compiler_params=_params(),
        name="sgu_v" if emit_v else "sgu",
    )(x, g, w_in, b_in, ln_g, ln_b, ws, bsb, w_out, b_out)


def _conv_kernel(x_ref, g_ref, w1_ref, b1_ref, past_ref, wdw_ref, bdw_ref, lng_ref, lnb_ref, w2_ref, b2_ref,
                 o_ref, st_ref, pad_scr, conv_scr, *, nseg, seg_len, tiles_per_seq, row_block):
    x = x_ref[...]
    h = _rms(x, g_ref[...]).astype(BF16)

    first = pl.program_id(0) % tiles_per_seq == 0

    @pl.when(first)
    def _():
        pad_scr[:, HALO_PAD:HALO, :] = past_ref[...]

    for j in range(0, D_MODEL, COL_CHUNK):
        a1 = _dot(h, w1_ref[:, j:j + COL_CHUNK]) + b1_ref[:, j:j + COL_CHUNK]
        a2 = _dot(h, w1_ref[:, D_MODEL + j:D_MODEL + j + COL_CHUNK]) + b1_ref[:, D_MODEL + j:D_MODEL + j + COL_CHUNK]
        glu = a1 * jax.nn.sigmoid(a2)
        for s in range(nseg):
            pad_scr[s, HALO:HALO + seg_len, j:j + COL_CHUNK] = glu[s * seg_len:(s + 1) * seg_len]

    for s in range(nseg):
        st_ref[s] = pad_scr[s, HALO + seg_len - CONV_STATE:HALO + seg_len, :]

    def lane_tile(c, carry):
        c0 = pl.multiple_of(c * LANES, LANES)
        for s in range(nseg):
            for r0 in range(0, seg_len, row_block):
                acc = jnp.broadcast_to(bdw_ref[:, pl.ds(c0, LANES)], (row_block, LANES))
                for k in range(CONV_WIDTH):
                    acc = acc + wdw_ref[k:k + 1, pl.ds(c0, LANES)] * pad_scr[
                        s, r0 + k + HALO_PAD:r0 + k + HALO_PAD + row_block, pl.ds(c0, LANES)]
                conv_scr[s * seg_len + r0:s * seg_len + r0 + row_block, pl.ds(c0, LANES)] = acc
        return carry

    lax.fori_loop(0, D_MODEL // LANES, lane_tile, 0)

    if tiles_per_seq > 1:
        pad_scr[:, 0:HALO, :] = pad_scr[:, seg_len:seg_len + HALO, :]

    c = conv_scr[...]
    mu = jnp.mean(c, axis=-1, keepdims=True)
    cc = c - mu
    var = jnp.mean(cc * cc, axis=-1, keepdims=True)
    cn = cc * lax.rsqrt(var + LN_EPS) * lng_ref[...] + lnb_ref[...]
    act = jax.nn.silu(cn).astype(BF16)
    o_ref[...] = x + _dot(act, w2_ref[...]) + b2_ref[...]


def _conv(x, g, w1, b1, past, wdw, bdw, ln_g, ln_b, w2, b2, nseg, seg_len, tiles_per_seq):
    rows = x.shape[0]
    tile = nseg * seg_len
    nseq = past.shape[0]
    row_block = min(seg_len, 64)
    row_spec = pl.BlockSpec((tile, D_MODEL), lambda i: (i, 0))
    seq_spec = pl.BlockSpec((nseg, CONV_STATE, D_MODEL), lambda i: (i // tiles_per_seq, 0, 0))
    in_specs = [row_spec, _const_spec((1, D_MODEL)), _const_spec((D_MODEL, 2 * D_MODEL)),
                _const_spec((1, 2 * D_MODEL)), seq_spec, _const_spec((CONV_WIDTH, D_MODEL)),
                _const_spec((1, D_MODEL)), _const_spec((1, D_MODEL)), _const_spec((1, D_MODEL)),
                _const_spec((D_MODEL, D_MODEL)), _const_spec((1, D_MODEL))]
    return pl.pallas_call(
        functools.partial(_conv_kernel, nseg=nseg, seg_len=seg_len, tiles_per_seq=tiles_per_seq,
                          row_block=row_block),
        out_shape=(jax.ShapeDtypeStruct((rows, D_MODEL), F32),
                   jax.ShapeDtypeStruct((nseq, CONV_STATE, D_MODEL), F32)),
        grid=(rows // tile,),
        in_specs=in_specs,
        out_specs=(row_spec, seq_spec),
        scratch_shapes=[pltpu.VMEM((nseg, HALO + seg_len, D_MODEL), F32), pltpu.VMEM((tile, D_MODEL), F32)],
        compiler_params=_params(),
        name="conv",
    )(x, g, w1, b1, past, wdw, bdw, ln_g, ln_b, w2, b2)


def _trunk(x3, conv_past, w, sgu_tile, conv_nseg, conv_seg_len, ffn_tile, emit_v):
    nb, t, _ = x3.shape
    x = x3.reshape(nb * t, D_MODEL)
    chunk_len = min(t, SGU_CHUNK)
    conv_states, sgu_vs = [], []
    for i in range(DEPTH):
        j = i // 2
        g_mix = w["norm_mix_g"][i][None]
        if i % 2 == 0:
            ws = w["sgu_w_s"][j][:, :chunk_len, :chunk_len]
            bsb = jnp.broadcast_to(w["sgu_b_s"][j][:, :chunk_len, None], (SGU_HEADS, chunk_len, LANES))
            res = _sgu(x, g_mix, w["sgu_w_in"][j], w["sgu_b_in"][j][None], w["sgu_ln_g"][j][None],
                       w["sgu_ln_b"][j][None], ws, bsb, w["sgu_w_out"][j], w["sgu_b_out"][j][None],
                       sgu_tile, chunk_len, emit_v)
            if emit_v:
                x, v = res
                sgu_vs.append(v.reshape(nb, t, D_SGU))
            else:
                x = res
        else:
            x, st = _conv(x, g_mix, w["conv_w_pw1"][j], w["conv_b_pw1"][j][None], conv_past[j],
                          w["conv_w_dw"][j], w["conv_b_dw"][j][None], w["conv_ln_g"][j][None],
                          w["conv_ln_b"][j][None], w["conv_w_pw2"][j], w["conv_b_pw2"][j][None],
                          conv_nseg, conv_seg_len, t // conv_seg_len)
            conv_states.append(st)
        gf = w["norm_final_g"][None] if i == DEPTH - 1 else None
        x = _ffn(x, w["norm_ffn_g"][i][None], w["ffn_w_gate"][i], w["ffn_w_up"][i], w["ffn_w_down"][i], gf,
                 ffn_tile)
    return x.reshape(nb, t, D_MODEL), jnp.stack(conv_states), (jnp.stack(sgu_vs) if emit_v else None)


def kernel(x_prompt, x_sample, state_conv, norm_mix_g, norm_ffn_g, norm_final_g, sgu_w_in, sgu_b_in, sgu_ln_g, sgu_ln_b, sgu_w_s, sgu_b_s, sgu_w_out, sgu_b_out, conv_w_pw1, conv_b_pw1, conv_w_dw, conv_b_dw, conv_ln_g, conv_ln_b, conv_w_pw2, conv_b_pw2, ffn_w_gate, ffn_w_up, ffn_w_down):
    w = dict(norm_mix_g=norm_mix_g, norm_ffn_g=norm_ffn_g, norm_final_g=norm_final_g,
             sgu_w_in=sgu_w_in.astype(BF16), sgu_b_in=sgu_b_in, sgu_ln_g=sgu_ln_g, sgu_ln_b=sgu_ln_b,
             sgu_w_s=sgu_w_s, sgu_b_s=sgu_b_s, sgu_w_out=sgu_w_out.astype(BF16), sgu_b_out=sgu_b_out,
             conv_w_pw1=conv_w_pw1.astype(BF16), conv_b_pw1=conv_b_pw1, conv_w_dw=conv_w_dw, conv_b_dw=conv_b_dw,
             conv_ln_g=conv_ln_g, conv_ln_b=conv_ln_b, conv_w_pw2=conv_w_pw2.astype(BF16), conv_b_pw2=conv_b_pw2,
             ffn_w_gate=ffn_w_gate.astype(BF16), ffn_w_up=ffn_w_up.astype(BF16),
             ffn_w_down=ffn_w_down.astype(BF16))
    nb_p, t_p, _ = x_prompt.shape
    nb_s, t_s, _ = x_sample.shape
    zero_past = jnp.zeros((DEPTH // 2, nb_p, CONV_STATE, D_MODEL), F32)
    y_p, conv_p, _ = _trunk(x_prompt, zero_past, w, PROMPT_TILE_SGU, 1, PROMPT_TILE_CONV, PROMPT_TILE_FFN, False)
    y_s, conv_s, v_s = _trunk(x_sample, state_conv, w, nb_s * t_s, nb_s, t_s, nb_s * t_s, True)
    return (y_p, y_s, conv_p, conv_s, v_s)
```

```python
import functools

import jax
import jax.numpy as jnp
from jax import lax
from jax.experimental import pallas as pl
from jax.experimental.pallas import tpu as pltpu

D_MODEL = 1024
DEPTH = 4
SGU_CHUNK = 128
SGU_HEADS = 4
D_SGU = 3 * D_MODEL
SGU_HEAD_DIM = D_SGU // SGU_HEADS
CONV_WIDTH = 31
CONV_STATE = CONV_WIDTH - 1
D_FF = 2816
RMS_EPS = 1e-6
LN_EPS = 1e-5

LANES = 128
HALO = 32
HALO_PAD = HALO - CONV_STATE
COL_CHUNK = 256
VMEM_LIMIT_BYTES = 56 * 1024 * 1024

PROMPT_TILE_SGU = 512
PROMPT_TILE_CONV = 512
PROMPT_TILE_FFN = 512

F32 = jnp.float32
BF16 = jnp.bfloat16


def _rms(x, g):
    return x * lax.rsqrt(jnp.mean(x * x, axis=-1, keepdims=True) + RMS_EPS) * g


def _dot(a, b):
    return jnp.dot(a, b, preferred_element_type=F32)


def _lane_partial(v):
    acc = v[:, :LANES]
    for k in range(LANES, v.shape[1], LANES):
        acc = acc + v[:, k:k + LANES]
    return acc


def _ffn_kernel(*refs, final):
    if final:
        x_ref, g_ref, wg_ref, wu_ref, wd_ref, gf_ref, o_ref, act_ref = refs
    else:
        x_ref, g_ref, wg_ref, wu_ref, wd_ref, o_ref, act_ref = refs
    x = x_ref[...]
    h = _rms(x, g_ref[...]).astype(BF16)
    for j in range(0, D_FF, COL_CHUNK):
        g = _dot(h, wg_ref[:, j:j + COL_CHUNK])
        u = _dot(h, wu_ref[:, j:j + COL_CHUNK])
        act_ref[:, j:j + COL_CHUNK] = (jax.nn.silu(g) * u).astype(BF16)
    xo = x + _dot(act_ref[...], wd_ref[...])
    if final:
        xo = _rms(xo, gf_ref[...])
    o_ref[...] = xo


def _const_spec(shape):
    nd = len(shape)
    return pl.BlockSpec(shape, lambda i: (0,) * nd, pipeline_mode=pl.Buffered(1))


def _params(flags=None):
    return pltpu.CompilerParams(dimension_semantics=("arbitrary",),
                                vmem_limit_bytes=VMEM_LIMIT_BYTES, flags=flags)


def _ffn(x, g, wg, wu, wd, gf, tile):
    rows = x.shape[0]
    final = gf is not None
    row_spec = pl.BlockSpec((tile, D_MODEL), lambda i: (i, 0))
    in_specs = [row_spec, _const_spec((1, D_MODEL)), _const_spec((D_MODEL, D_FF)),
                _const_spec((D_MODEL, D_FF)), _const_spec((D_FF, D_MODEL))]
    args = [x, g, wg, wu, wd]
    if final:
        in_specs.append(_const_spec((1, D_MODEL)))
        args.append(gf)
    return pl.pallas_call(
        functools.partial(_ffn_kernel, final=final),
        out_shape=jax.ShapeDtypeStruct((rows, D_MODEL), F32),
        grid=(rows // tile,),
        in_specs=in_specs,
        out_specs=row_spec,
        scratch_shapes=[pltpu.VMEM((tile, D_FF), BF16)],
        compiler_params=_params(),
        name="ffn_final" if final else "ffn",
    )(*args)


def _sgu_kernel(*refs, chunk_len, emit_v):
    if emit_v:
        (x_ref, g_ref, win_ref, bin_ref, lng_ref, lnb_ref, ws_ref, bsb_ref, wout_ref, bout_ref,
         o_ref, v_ref, u_scr, v_scr, vn_scr, gated_scr) = refs
    else:
        (x_ref, g_ref, win_ref, bin_ref, lng_ref, lnb_ref, ws_ref, bsb_ref, wout_ref, bout_ref,
         o_ref, u_scr, v_scr, vn_scr, gated_scr) = refs
    tile = x_ref.shape[0]
    x = x_ref[...]
    h = _rms(x, g_ref[...]).astype(BF16)

    s1 = jnp.zeros((tile, LANES), F32)
    s2 = jnp.zeros((tile, LANES), F32)
    for j in range(0, D_SGU, COL_CHUNK):
        z = _dot(h, win_ref[:, D_SGU + j:D_SGU + j + COL_CHUNK]) + bin_ref[:, D_SGU + j:D_SGU + j + COL_CHUNK]
        v = jax.nn.gelu(z, approximate=True)
        s1 = s1 + _lane_partial(v)
        s2 = s2 + _lane_partial(v * v)
        v_scr[:, j:j + COL_CHUNK] = v
    mean = jnp.sum(s1, axis=-1, keepdims=True) * (1.0 / D_SGU)
    var = jnp.sum(s2, axis=-1, keepdims=True) * (1.0 / D_SGU) - mean * mean
    rstd = lax.rsqrt(var + LN_EPS)
    for j in range(0, D_SGU, COL_CHUNK):
        z = _dot(h, win_ref[:, j:j + COL_CHUNK]) + bin_ref[:, j:j + COL_CHUNK]
        u_scr[:, j:j + COL_CHUNK] = jax.nn.gelu(z, approximate=True)
        vn = (v_scr[:, j:j + COL_CHUNK] - mean) * rstd * lng_ref[:, j:j + COL_CHUNK] + lnb_ref[:, j:j + COL_CHUNK]
        if emit_v:
            v_ref[:, j:j + COL_CHUNK] = vn
        vn_scr[:, j:j + COL_CHUNK] = vn.astype(BF16)

    o_ref[...] = x + bout_ref[...]
    row = lax.broadcasted_iota(jnp.int32, (chunk_len, chunk_len), 0)
    col = lax.broadcasted_iota(jnp.int32, (chunk_len, chunk_len), 1)
    for hd in range(SGU_HEADS):
        w = jnp.where(row >= col, ws_ref[hd], 0.0).astype(BF16)
        bias = jnp.concatenate([bsb_ref[hd]] * (SGU_HEAD_DIM // LANES), axis=1)
        c0 = hd * SGU_HEAD_DIM
        for r0 in range(0, tile, chunk_len):
            mixed = _dot(w, vn_scr[r0:r0 + chunk_len, c0:c0 + SGU_HEAD_DIM]) + bias
            gated_scr[r0:r0 + chunk_len, c0:c0 + SGU_HEAD_DIM] = (
                u_scr[r0:r0 + chunk_len, c0:c0 + SGU_HEAD_DIM] * mixed).astype(BF16)
        o_ref[...] += _dot(gated_scr[:, c0:c0 + SGU_HEAD_DIM], wout_ref[c0:c0 + SGU_HEAD_DIM, :])


def _sgu(x, g, w_in, b_in, ln_g, ln_b, ws, bsb, w_out, b_out, tile, chunk_len, emit_v):
    rows = x.shape[0]
    row_spec = pl.BlockSpec((tile, D_MODEL), lambda i: (i, 0))
    in_specs = [row_spec, _const_spec((1, D_MODEL)), _const_spec((D_MODEL, 2 * D_SGU)),
                _const_spec((1, 2 * D_SGU)), _const_spec((1, D_SGU)), _const_spec((1, D_SGU)),
                _const_spec((SGU_HEADS, chunk_len, chunk_len)), _const_spec((SGU_HEADS, chunk_len, LANES)),
                _const_spec((D_SGU, D_MODEL)), _const_spec((1, D_MODEL))]
    out_shape = jax.ShapeDtypeStruct((rows, D_MODEL), F32)
    out_specs = row_spec
    if emit_v:
        out_shape = (out_shape, jax.ShapeDtypeStruct((rows, D_SGU), F32))
        out_specs = (row_spec, pl.BlockSpec((tile, D_SGU), lambda i: (i, 0)))
    return pl.pallas_call(
        functools.partial(_sgu_kernel, chunk_len=chunk_len, emit_v=emit_v),
        out_shape=out_shape,
        grid=(rows // tile,),
        in_specs=in_specs,
        out_specs=out_specs,
        scratch_shapes=[pltpu.VMEM((tile, D_SGU), F32), pltpu.VMEM((tile, D_SGU), F32),
                        pltpu.VMEM((tile, D_SGU), BF16), pltpu.VMEM((tile, D_SGU), BF16)],
        compiler_params=_params(),
        name="sgu_v" if emit_v else "sgu",
    )(x, g, w_in, b_in, ln_g, ln_b, ws, bsb, w_out, b_out)


def _conv_kernel(x_ref, g_ref, w1_ref, b1_ref, past_ref, wdw_ref, bdw_ref, lng_ref, lnb_ref, w2_ref, b2_ref,
                 o_ref, st_ref, pad_scr, conv_scr, *, nseg, seg_len, tiles_per_seq, row_block):
    n_lane_tiles = D_MODEL // LANES
    x = x_ref[...]
    h = _rms(x, g_ref[...]).astype(BF16)

    first = pl.program_id(0) % tiles_per_seq == 0

    @pl.when(first)
    def _():
        for l in range(n_lane_tiles):
            pad_scr[:, l, HALO_PAD:HALO, :] = past_ref[:, :, l * LANES:(l + 1) * LANES]

    for j in range(0, D_MODEL, COL_CHUNK):
        a1 = _dot(h, w1_ref[:, j:j + COL_CHUNK]) + b1_ref[:, j:j + COL_CHUNK]
        a2 = _dot(h, w1_ref[:, D_MODEL + j:D_MODEL + j + COL_CHUNK]) + b1_ref[:, D_MODEL + j:D_MODEL + j + COL_CHUNK]
        glu = a1 * jax.nn.sigmoid(a2)
        for s in range(nseg):
            for jj in range(0, COL_CHUNK, LANES):
                pad_scr[s, (j + jj) // LANES, HALO:HALO + seg_len, :] = glu[s * seg_len:(s + 1) * seg_len,
                                                                            jj:jj + LANES]

    for s in range(nseg):
        st_ref[s] = jnp.concatenate(
            [pad_scr[s, l, HALO + seg_len - CONV_STATE:HALO + seg_len, :] for l in range(n_lane_tiles)], axis=1)

    def lane_tile(l, carry):
        for s in range(nseg):
            for r0 in range(0, seg_len, row_block):
                acc = jnp.broadcast_to(bdw_ref[l], (row_block, LANES))
                for k in range(CONV_WIDTH):
                    acc = acc + wdw_ref[l, k:k + 1, :] * pad_scr[
                        s, l, r0 + k + HALO_PAD:r0 + k + HALO_PAD + row_block, :]
                conv_scr[l, s * seg_len + r0:s * seg_len + r0 + row_block, :] = acc
        return carry

    lax.fori_loop(0, n_lane_tiles, lane_tile, 0)

    if tiles_per_seq > 1:
        pad_scr[:, :, 0:HALO, :] = pad_scr[:, :, seg_len:seg_len + HALO, :]

    c = jnp.concatenate([conv_scr[l] for l in range(n_lane_tiles)], axis=1)
    mu = jnp.mean(c, axis=-1, keepdims=True)
    cc = c - mu
    var = jnp.mean(cc * cc, axis=-1, keepdims=True)
    cn = cc * lax.rsqrt(var + LN_EPS) * lng_ref[...] + lnb_ref[...]
    act = jax.nn.silu(cn).astype(BF16)
    o_ref[...] = x + _dot(act, w2_ref[...]) + b2_ref[...]


def _lane_tile_major(a):
    return a.reshape(a.shape[0], D_MODEL // LANES, LANES).transpose(1, 0, 2)


def _conv(x, g, w1, b1, past, wdw, bdw, ln_g, ln_b, w2, b2, nseg, seg_len, tiles_per_seq):
    rows = x.shape[0]
    tile = nseg * seg_len
    nseq = past.shape[0]
    n_lane_tiles = D_MODEL // LANES
    row_block = min(seg_len, 64)
    row_spec = pl.BlockSpec((tile, D_MODEL), lambda i: (i, 0))
    seq_spec = pl.BlockSpec((nseg, CONV_STATE, D_MODEL), lambda i: (i // tiles_per_seq, 0, 0))
    in_specs = [row_spec, _const_spec((1, D_MODEL)), _const_spec((D_MODEL, 2 * D_MODEL)),
                _const_spec((1, 2 * D_MODEL)), seq_spec, _const_spec((n_lane_tiles, CONV_WIDTH, LANES)),
                _const_spec((n_lane_tiles, 1, LANES)), _const_spec((1, D_MODEL)), _const_spec((1, D_MODEL)),
                _const_spec((D_MODEL, D_MODEL)), _const_spec((1, D_MODEL))]
    return pl.pallas_call(
        functools.partial(_conv_kernel, nseg=nseg, seg_len=seg_len, tiles_per_seq=tiles_per_seq,
                          row_block=row_block),
        out_shape=(jax.ShapeDtypeStruct((rows, D_MODEL), F32),
                   jax.ShapeDtypeStruct((nseq, CONV_STATE, D_MODEL), F32)),
        grid=(rows // tile,),
        in_specs=in_specs,
        out_specs=(row_spec, seq_spec),
        scratch_shapes=[pltpu.VMEM((nseg, n_lane_tiles, HALO + seg_len, LANES), F32),
                        pltpu.VMEM((n_lane_tiles, tile, LANES), F32)],
        compiler_params=_params(),
        name="conv",
    )(x, g, w1, b1, past, _lane_tile_major(wdw), _lane_tile_major(bdw), ln_g, ln_b, w2, b2)


def _trunk(x3, conv_past, w, sgu_tile, conv_nseg, conv_seg_len, ffn_tile, emit_v):
    nb, t, _ = x3.shape
    x = x3.reshape(nb * t, D_MODEL)
    chunk_len = min(t, SGU_CHUNK)
    conv_states, sgu_vs = [], []
    for i in range(DEPTH):
        j = i // 2
        g_mix = w["norm_mix_g"][i][None]
        if i % 2 == 0:
            ws = w["sgu_w_s"][j][:, :chunk_len, :chunk_len]
            bsb = jnp.broadcast_to(w["sgu_b_s"][j][:, :chunk_len, None], (SGU_HEADS, chunk_len, LANES))
            res = _sgu(x, g_mix, w["sgu_w_in"][j], w["sgu_b_in"][j][None], w["sgu_ln_g"][j][None],
                       w["sgu_ln_b"][j][None], ws, bsb, w["sgu_w_out"][j], w["sgu_b_out"][j][None],
                       sgu_tile, chunk_len, emit_v)
            if emit_v:
                x, v = res
                sgu_vs.append(v.reshape(nb, t, D_SGU))
            else:
                x = res
        else:
            x, st = _conv(x, g_mix, w["conv_w_pw1"][j], w["conv_b_pw1"][j][None], conv_past[j],
                          w["conv_w_dw"][j], w["conv_b_dw"][j][None], w["conv_ln_g"][j][None],
                          w["conv_ln_b"][j][None], w["conv_w_pw2"][j], w["conv_b_pw2"][j][None],
                          conv_nseg, conv_seg_len, t // conv_seg_len)
            conv_states.append(st)
        gf = w["norm_final_g"][None] if i == DEPTH - 1 else None
        x = _ffn(x, w["norm_ffn_g"][i][None], w["ffn_w_gate"][i], w["ffn_w_up"][i], w["ffn_w_down"][i], gf,
                 ffn_tile)
    return x.reshape(nb, t, D_MODEL), jnp.stack(conv_states), (jnp.stack(sgu_vs) if emit_v else None)


def kernel(x_prompt, x_sample, state_conv, norm_mix_g, norm_ffn_g, norm_final_g, sgu_w_in, sgu_b_in, sgu_ln_g, sgu_ln_b, sgu_w_s, sgu_b_s, sgu_w_out, sgu_b_out, conv_w_pw1, conv_b_pw1, conv_w_dw, conv_b_dw, conv_ln_g, conv_ln_b, conv_w_pw2, conv_b_pw2, ffn_w_gate, ffn_w_up, ffn_w_down):
    w = dict(norm_mix_g=norm_mix_g, norm_ffn_g=norm_ffn_g, norm_final_g=norm_final_g,
             sgu_w_in=sgu_w_in.astype(BF16), sgu_b_in=sgu_b_in, sgu_ln_g=sgu_ln_g, sgu_ln_b=sgu_ln_b,
             sgu_w_s=sgu_w_s, sgu_b_s=sgu_b_s, sgu_w_out=sgu_w_out.astype(BF16), sgu_b_out=sgu_b_out,
             conv_w_pw1=conv_w_pw1.astype(BF16), conv_b_pw1=conv_b_pw1, conv_w_dw=conv_w_dw, conv_b_dw=conv_b_dw,
             conv_ln_g=conv_ln_g, conv_ln_b=conv_ln_b, conv_w_pw2=conv_w_pw2.astype(BF16), conv_b_pw2=conv_b_pw2,
             ffn_w_gate=ffn_w_gate.astype(BF16), ffn_w_up=ffn_w_up.astype(BF16),
             ffn_w_down=ffn_w_down.astype(BF16))
    nb_p, t_p, _ = x_prompt.shape
    nb_s, t_s, _ = x_sample.shape
    zero_past = jnp.zeros((DEPTH // 2, nb_p, CONV_STATE, D_MODEL), F32)
    y_p, conv_p, _ = _trunk(x_prompt, zero_past, w, PROMPT_TILE_SGU, 1, PROMPT_TILE_CONV, PROMPT_TILE_FFN, False)
    y_s, conv_s, v_s = _trunk(x_sample, state_conv, w, nb_s * t_s, nb_s, t_s, nb_s * t_s, True)
    return (y_p, y_s, conv_p, conv_s, v_s)
```

```python
import functools
import math

import jax
import jax.numpy as jnp
from jax import lax
from jax.experimental import pallas as pl
from jax.experimental.pallas import tpu as pltpu

D_MODEL = 1024
DEPTH = 4
SGU_CHUNK = 128
SGU_HEADS = 4
D_SGU = 3 * D_MODEL
SGU_HEAD_DIM = D_SGU // SGU_HEADS
CONV_WIDTH = 31
CONV_STATE = CONV_WIDTH - 1
D_FF = 2816
RMS_EPS = 1e-6
LN_EPS = 1e-5

LANES = 128
N_LANE_TILES = D_MODEL // LANES
HALO = 32
HALO_PAD = HALO - CONV_STATE
COL_CHUNK = 256
CONV_ROW_BLOCK = 64
VMEM_LIMIT_BYTES = 56 * 1024 * 1024

PROMPT_TILE = 512

F32 = jnp.float32
BF16 = jnp.bfloat16

_GELU_C = math.sqrt(2.0 / math.pi)


def _rms(x, g):
    return x * lax.rsqrt(jnp.mean(x * x, axis=-1, keepdims=True) + RMS_EPS) * g


def _gelu(z):
    w = (z * z) * (_GELU_C * 0.044715) + _GELU_C
    hz = 0.5 * z
    return hz + hz * jnp.tanh(z * w)


def _dot(a, b):
    return jnp.dot(a, b, preferred_element_type=F32)


def _lane_partial(v):
    acc = v[:, :LANES]
    for k in range(LANES, v.shape[1], LANES):
        acc = acc + v[:, k:k + LANES]
    return acc


def _layer_spec(layer, shape):
    nd = len(shape)
    return pl.BlockSpec((None,) + tuple(shape), lambda i: (layer,) + (0,) * nd, pipeline_mode=pl.Buffered(1))


def _const_spec(shape):
    nd = len(shape)
    return pl.BlockSpec(shape, lambda i: (0,) * nd, pipeline_mode=pl.Buffered(1))


def _params():
    return pltpu.CompilerParams(dimension_semantics=("arbitrary",), vmem_limit_bytes=VMEM_LIMIT_BYTES)


def _ffn_chunk(h, wg_ref, wu_ref, act_ref, j):
    g = _dot(h, wg_ref[:, j:j + COL_CHUNK])
    u = _dot(h, wu_ref[:, j:j + COL_CHUNK])
    act_ref[:, j:j + COL_CHUNK] = (jax.nn.silu(g) * u).astype(BF16)


def _ffn_kernel(*refs, final):
    if final:
        x_ref, g_ref, wg_ref, wu_ref, wd_ref, gf_ref, o_ref, act_ref = refs
    else:
        x_ref, g_ref, wg_ref, wu_ref, wd_ref, o_ref, act_ref = refs
    x = x_ref[...]
    h = _rms(x, g_ref[...]).astype(BF16)
    for j in range(0, D_FF, COL_CHUNK):
        _ffn_chunk(h, wg_ref, wu_ref, act_ref, j)
    xo = x + _dot(act_ref[...], wd_ref[...])
    if final:
        xo = _rms(xo, gf_ref[...])
    o_ref[...] = xo


def _ffn_specs(layer):
    return [_layer_spec(layer, (1, D_MODEL)), _layer_spec(layer, (D_MODEL, D_FF)),
            _layer_spec(layer, (D_MODEL, D_FF)), _layer_spec(layer, (D_FF, D_MODEL))]


def _ffn(x, w, layer, tile):
    rows = x.shape[0]
    final = layer == DEPTH - 1
    row_spec = pl.BlockSpec((tile, D_MODEL), lambda i: (i, 0))
    in_specs = [row_spec] + _ffn_specs(layer)
    args = [x, w["norm_ffn_g"], w["ffn_w_gate"], w["ffn_w_up"], w["ffn_w_down"]]
    if final:
        in_specs.append(_const_spec((1, D_MODEL)))
        args.append(w["norm_final_g"])
    return pl.pallas_call(
        functools.partial(_ffn_kernel, final=final),
        out_shape=jax.ShapeDtypeStruct((rows, D_MODEL), F32),
        grid=(rows // tile,),
        in_specs=in_specs,
        out_specs=row_spec,
        scratch_shapes=[pltpu.VMEM((tile, D_FF), BF16)],
        compiler_params=_params(),
        name="ffn_final" if final else "ffn",
    )(*args)


def _sgu_kernel(*refs, chunk_len, emit_v):
    if emit_v:
        (x_ref, g_ref, win_ref, bin_ref, lng_ref, lnb_ref, ws_ref, bsb_ref, wout_ref, bout_ref,
         o_ref, v_ref, u_scr, v_scr, vn_scr, gated_scr) = refs
    else:
        (x_ref, g_ref, win_ref, bin_ref, lng_ref, lnb_ref, ws_ref, bsb_ref, wout_ref, bout_ref,
         o_ref, u_scr, v_scr, vn_scr, gated_scr) = refs
    tile = x_ref.shape[0]
    h = _rms(x_ref[...], g_ref[...]).astype(BF16)

    s1 = jnp.zeros((tile, LANES), F32)
    s2 = jnp.zeros((tile, LANES), F32)
    for j in range(0, D_SGU, COL_CHUNK):
        z = _dot(h, win_ref[:, D_SGU + j:D_SGU + j + COL_CHUNK]) + bin_ref[:, D_SGU + j:D_SGU + j + COL_CHUNK]
        v = _gelu(z)
        s1 = s1 + _lane_partial(v)
        s2 = s2 + _lane_partial(v * v)
        v_scr[:, j:j + COL_CHUNK] = v
    mean = jnp.sum(s1, axis=-1, keepdims=True) * (1.0 / D_SGU)
    var = jnp.sum(s2, axis=-1, keepdims=True) * (1.0 / D_SGU) - mean * mean
    rstd = lax.rsqrt(var + LN_EPS)
    for j in range(0, D_SGU, COL_CHUNK):
        u_scr[:, j:j + COL_CHUNK] = _dot(h, win_ref[:, j:j + COL_CHUNK]) + bin_ref[:, j:j + COL_CHUNK]
        vn = (v_scr[:, j:j + COL_CHUNK] - mean) * rstd * lng_ref[:, j:j + COL_CHUNK] + lnb_ref[:, j:j + COL_CHUNK]
        if emit_v:
            v_ref[:, j:j + COL_CHUNK] = vn
        vn_scr[:, j:j + COL_CHUNK] = vn.astype(BF16)

    o_ref[...] = x_ref[...] + bout_ref[...]
    row = lax.broadcasted_iota(jnp.int32, (chunk_len, chunk_len), 0)
    col = lax.broadcasted_iota(jnp.int32, (chunk_len, chunk_len), 1)
    for hd in range(SGU_HEADS):
        w = jnp.where(row >= col, ws_ref[hd], 0.0).astype(BF16)
        bias = jnp.concatenate([bsb_ref[hd]] * (SGU_HEAD_DIM // LANES), axis=1)
        c0 = hd * SGU_HEAD_DIM
        for r0 in range(0, tile, chunk_len):
            mixed = _dot(w, vn_scr[r0:r0 + chunk_len, c0:c0 + SGU_HEAD_DIM]) + bias
            gated_scr[r0:r0 + chunk_len, c0:c0 + SGU_HEAD_DIM] = (
                _gelu(u_scr[r0:r0 + chunk_len, c0:c0 + SGU_HEAD_DIM]) * mixed).astype(BF16)
        o_ref[...] += _dot(gated_scr[:, c0:c0 + SGU_HEAD_DIM], wout_ref[c0:c0 + SGU_HEAD_DIM, :])


def _sgu(x, w, layer, tile, chunk_len, emit_v):
    rows = x.shape[0]
    j = layer // 2
    row_spec = pl.BlockSpec((tile, D_MODEL), lambda i: (i, 0))
    in_specs = [row_spec, _layer_spec(layer, (1, D_MODEL)), _layer_spec(j, (D_MODEL, 2 * D_SGU)),
                _layer_spec(j, (1, 2 * D_SGU)), _layer_spec(j, (1, D_SGU)), _layer_spec(j, (1, D_SGU)),
                _layer_spec(j, (SGU_HEADS, chunk_len, chunk_len)), _layer_spec(j, (SGU_HEADS, chunk_len, LANES)),
                _layer_spec(j, (D_SGU, D_MODEL)), _layer_spec(j, (1, D_MODEL))]
    out_shape = jax.ShapeDtypeStruct((rows, D_MODEL), F32)
    out_specs = row_spec
    if emit_v:
        out_shape = (out_shape, jax.ShapeDtypeStruct((rows, D_SGU), F32))
        out_specs = (row_spec, pl.BlockSpec((tile, D_SGU), lambda i: (i, 0)))
    ws = w["sgu_w_s"][:, :, :chunk_len, :chunk_len]
    bsb = jnp.broadcast_to(w["sgu_b_s"][:, :, :chunk_len, None], ws.shape[:3] + (LANES,))
    return pl.pallas_call(
        functools.partial(_sgu_kernel, chunk_len=chunk_len, emit_v=emit_v),
        out_shape=out_shape,
        grid=(rows // tile,),
        in_specs=in_specs,
        out_specs=out_specs,
        scratch_shapes=[pltpu.VMEM((tile, D_SGU), F32), pltpu.VMEM((tile, D_SGU), F32),
                        pltpu.VMEM((tile, D_SGU), BF16), pltpu.VMEM((tile, D_SGU), BF16)],
        compiler_params=_params(),
        name="sgu_v" if emit_v else "sgu",
    )(x, w["norm_mix_g"], w["sgu_w_in"], w["sgu_b_in"], w["sgu_ln_g"], w["sgu_ln_b"], ws, bsb,
      w["sgu_w_out"], w["sgu_b_out"])


def _conv_mixer(x, g_ref, w1_ref, b1_ref, wdw_ref, bdw_ref, lng_ref, lnb_ref, w2_ref, b2_ref, st_ref,
                pad_scr, conv_scr, *, nseg, seg_len, carry, between=None):
    row_block = min(seg_len, CONV_ROW_BLOCK)
    h = _rms(x, g_ref[...]).astype(BF16)
    for j in range(0, D_MODEL, COL_CHUNK):
        a1 = _dot(h, w1_ref[:, j:j + COL_CHUNK]) + b1_ref[:, j:j + COL_CHUNK]
        a2 = _dot(h, w1_ref[:, D_MODEL + j:D_MODEL + j + COL_CHUNK]) + b1_ref[:, D_MODEL + j:D_MODEL + j + COL_CHUNK]
        glu = a1 * jax.nn.sigmoid(a2)
        for s in range(nseg):
            for jj in range(0, COL_CHUNK, LANES):
                pad_scr[s, (j + jj) // LANES, HALO:HALO + seg_len, :] = glu[s * seg_len:(s + 1) * seg_len,
                                                                            jj:jj + LANES]
        for l in range(j // LANES, (j + COL_CHUNK) // LANES):
            for s in range(nseg):
                for r0 in range(0, seg_len, row_block):
                    acc = jnp.broadcast_to(bdw_ref[l], (row_block, LANES))
                    for k in range(CONV_WIDTH):
                        acc = acc + wdw_ref[l, k:k + 1, :] * pad_scr[
                            s, l, r0 + k + HALO_PAD:r0 + k + HALO_PAD + row_block, :]
                    conv_scr[l, s * seg_len + r0:s * seg_len + r0 + row_block, :] = acc
                    if between is not None:
                        between(l, s * seg_len + r0)

    for s in range(nseg):
        st_ref[s] = jnp.concatenate(
            [pad_scr[s, l, HALO + seg_len - CONV_STATE:HALO + seg_len, :] for l in range(N_LANE_TILES)], axis=1)
    if carry:
        pad_scr[:, :, 0:HALO, :] = pad_scr[:, :, seg_len:seg_len + HALO, :]

    c = jnp.concatenate([conv_scr[l] for l in range(N_LANE_TILES)], axis=1)
    mu = jnp.mean(c, axis=-1, keepdims=True)
    cc = c - mu
    var = jnp.mean(cc * cc, axis=-1, keepdims=True)
    cn = cc * lax.rsqrt(var + LN_EPS) * lng_ref[...] + lnb_ref[...]
    act = jax.nn.silu(cn).astype(BF16)
    return x + _dot(act, w2_ref[...]) + b2_ref[...]


def _conv_kernel(x_ref, g_ref, w1_ref, b1_ref, past_ref, wdw_ref, bdw_ref, lng_ref, lnb_ref, w2_ref, b2_ref,
                 o_ref, st_ref, pad_scr, conv_scr, *, nseg, seg_len):
    for l in range(N_LANE_TILES):
        pad_scr[:, l, HALO_PAD:HALO, :] = past_ref[:, :, l * LANES:(l + 1) * LANES]
    o_ref[...] = _conv_mixer(x_ref[...], g_ref, w1_ref, b1_ref, wdw_ref, bdw_ref, lng_ref, lnb_ref, w2_ref, b2_ref,
                             st_ref, pad_scr, conv_scr, nseg=nseg, seg_len=seg_len, carry=False)


def _conv_specs(layer):
    j = layer // 2
    return [_layer_spec(layer, (1, D_MODEL)), _layer_spec(j, (D_MODEL, 2 * D_MODEL)), _layer_spec(j, (1, 2 * D_MODEL)),
            _layer_spec(j, (N_LANE_TILES, CONV_WIDTH, LANES)), _layer_spec(j, (N_LANE_TILES, 1, LANES)),
            _layer_spec(j, (1, D_MODEL)), _layer_spec(j, (1, D_MODEL)), _layer_spec(j, (D_MODEL, D_MODEL)),
            _layer_spec(j, (1, D_MODEL))]


def _conv_args(w):
    return [w["norm_mix_g"], w["conv_w_pw1"], w["conv_b_pw1"], w["conv_w_dw"], w["conv_b_dw"],
            w["conv_ln_g"], w["conv_ln_b"], w["conv_w_pw2"], w["conv_b_pw2"]]


def _conv(x, past, w, layer, nseg, seg_len):
    rows = x.shape[0]
    assert rows == nseg * seg_len
    specs = _conv_specs(layer)
    in_specs = [_const_spec((rows, D_MODEL))] + specs[:3] + [_const_spec((nseg, CONV_STATE, D_MODEL))] + specs[3:]
    args = _conv_args(w)
    return pl.pallas_call(
        functools.partial(_conv_kernel, nseg=nseg, seg_len=seg_len),
        out_shape=(jax.ShapeDtypeStruct((rows, D_MODEL), F32),
                   jax.ShapeDtypeStruct((nseg, CONV_STATE, D_MODEL), F32)),
        grid=(1,),
        in_specs=in_specs,
        out_specs=(_const_spec((rows, D_MODEL)), _const_spec((nseg, CONV_STATE, D_MODEL))),
        scratch_shapes=[pltpu.VMEM((nseg, N_LANE_TILES, HALO + seg_len, LANES), F32),
                        pltpu.VMEM((N_LANE_TILES, rows, LANES), F32)],
        compiler_params=_params(),
        name="conv",
    )(x, *args[:3], past, *args[3:])


_FFN_SLOTS = ((0, 128), (0, 384), (1, 128), (2, 128), (2, 384), (3, 128), (4, 128), (4, 384), (5, 128),
              (6, 128), (7, 128))
assert len(_FFN_SLOTS) == D_FF // COL_CHUNK


def _conv_ffn_kernel(*refs, n_tiles, tiles_per_seq, final):
    (x_ref, gm_ref, w1_ref, b1_ref, wdw_ref, bdw_ref, lng_ref, lnb_ref, w2_ref, b2_ref,
     gf_ref, wg_ref, wu_ref, wd_ref) = refs[:14]
    rest = refs[14:]
    if final:
        gfin_ref, rest = rest[0], rest[1:]
    o_ref, st_ref, pad_scr, conv_scr, xmid_scr, xprev_scr, act_scr = rest
    i = pl.program_id(0)

    @pl.when(i == 0)
    def _():
        xmid_scr[...] = jnp.zeros(xmid_scr.shape, F32)

    xprev_scr[...] = xmid_scr[...]

    @pl.when(jnp.minimum(i, n_tiles - 1) % tiles_per_seq == 0)
    def _():
        pad_scr[:, :, 0:HALO, :] = jnp.zeros((1, N_LANE_TILES, HALO, LANES), F32)

    xp = xprev_scr[...]
    hf = _rms(xp, gf_ref[...]).astype(BF16)

    def between(l, r0):
        if (l, r0) in _FFN_SLOTS:
            _ffn_chunk(hf, wg_ref, wu_ref, act_scr, _FFN_SLOTS.index((l, r0)) * COL_CHUNK)

    xmid_scr[...] = _conv_mixer(x_ref[...], gm_ref, w1_ref, b1_ref, wdw_ref, bdw_ref, lng_ref, lnb_ref, w2_ref,
                                b2_ref, st_ref, pad_scr, conv_scr, nseg=1, seg_len=x_ref.shape[0], carry=True,
                                between=between)
    y = xp + _dot(act_scr[...], wd_ref[...])
    if final:
        y = _rms(y, gfin_ref[...])
    o_ref[...] = y


def _conv_ffn(x, w, layer, tile, seq_len):
    rows = x.shape[0]
    n_tiles = rows // tile
    tiles_per_seq = seq_len // tile
    final = layer == DEPTH - 1
    in_specs = ([pl.BlockSpec((tile, D_MODEL), lambda i: (jnp.minimum(i, n_tiles - 1), 0))]
                + _conv_specs(layer) + _ffn_specs(layer))
    args = [x] + _conv_args(w) + [w["norm_ffn_g"], w["ffn_w_gate"], w["ffn_w_up"], w["ffn_w_down"]]
    if final:
        in_specs.append(_const_spec((1, D_MODEL)))
        args.append(w["norm_final_g"])
    return pl.pallas_call(
        functools.partial(_conv_ffn_kernel, n_tiles=n_tiles, tiles_per_seq=tiles_per_seq, final=final),
        out_shape=(jax.ShapeDtypeStruct((rows, D_MODEL), F32),
                   jax.ShapeDtypeStruct((rows // seq_len, CONV_STATE, D_MODEL), F32)),
        grid=(n_tiles + 1,),
        in_specs=in_specs,
        out_specs=(pl.BlockSpec((tile, D_MODEL), lambda i: (jnp.maximum(i - 1, 0), 0)),
                   pl.BlockSpec((1, CONV_STATE, D_MODEL),
                                lambda i: (jnp.minimum(i, n_tiles - 1) // tiles_per_seq, 0, 0))),
        scratch_shapes=[pltpu.VMEM((1, N_LANE_TILES, HALO + tile, LANES), F32),
                        pltpu.VMEM((N_LANE_TILES, tile, LANES), F32),
                        pltpu.VMEM((tile, D_MODEL), F32), pltpu.VMEM((tile, D_MODEL), F32),
                        pltpu.VMEM((tile, D_FF), BF16)],
        compiler_params=_params(),
        name="conv_ffn_final" if final else "conv_ffn",
    )(*args)


def _lane_tile_major(a):
    return a.reshape(a.shape[0], a.shape[1], N_LANE_TILES, LANES).transpose(0, 2, 1, 3)


def _trunk_prompt(x3, w):
    nb, t, _ = x3.shape
    x = x3.reshape(nb * t, D_MODEL)
    conv_states = []
    for layer in range(DEPTH):
        if layer % 2 == 0:
            x = _sgu(x, w, layer, PROMPT_TILE, SGU_CHUNK, False)
            x = _ffn(x, w, layer, PROMPT_TILE)
        else:
            x, st = _conv_ffn(x, w, layer, PROMPT_TILE, t)
            conv_states.append(st)
    return x.reshape(nb, t, D_MODEL), jnp.stack(conv_states)


def _trunk_sample(x3, conv_past, w):
    nb, t, _ = x3.shape
    rows = nb * t
    x = x3.reshape(rows, D_MODEL)
    conv_states, sgu_vs = [], []
    for layer in range(DEPTH):
        if layer % 2 == 0:
            x, v = _sgu(x, w, layer, rows, t, True)
            sgu_vs.append(v.reshape(nb, t, D_SGU))
        else:
            x, st = _conv(x, conv_past[layer // 2], w, layer, nb, t)
            conv_states.append(st)
        x = _ffn(x, w, layer, rows)
    return x.reshape(nb, t, D_MODEL), jnp.stack(conv_states), jnp.stack(sgu_vs)


def kernel(x_prompt, x_sample, state_conv, norm_mix_g, norm_ffn_g, norm_final_g, sgu_w_in, sgu_b_in, sgu_ln_g, sgu_ln_b, sgu_w_s, sgu_b_s, sgu_w_out, sgu_b_out, conv_w_pw1, conv_b_pw1, conv_w_dw, conv_b_dw, conv_ln_g, conv_ln_b, conv_w_pw2, conv_b_pw2, ffn_w_gate, ffn_w_up, ffn_w_down):
    assert x_sample.shape[1] <= SGU_CHUNK and x_prompt.shape[1] % PROMPT_TILE == 0
    row = lambda a: a[:, None, :]
    w = dict(norm_mix_g=row(norm_mix_g), norm_ffn_g=row(norm_ffn_g), norm_final_g=norm_final_g[None],
             sgu_w_in=sgu_w_in.astype(BF16), sgu_b_in=row(sgu_b_in), sgu_ln_g=row(sgu_ln_g),
             sgu_ln_b=row(sgu_ln_b), sgu_w_s=sgu_w_s, sgu_b_s=sgu_b_s, sgu_w_out=sgu_w_out.astype(BF16),
             sgu_b_out=row(sgu_b_out),
             conv_w_pw1=conv_w_pw1.astype(BF16), conv_b_pw1=row(conv_b_pw1), conv_w_dw=_lane_tile_major(conv_w_dw),
             conv_b_dw=_lane_tile_major(row(conv_b_dw)), conv_ln_g=row(conv_ln_g), conv_ln_b=row(conv_ln_b),
             conv_w_pw2=conv_w_pw2.astype(BF16), conv_b_pw2=row(conv_b_pw2),
             ffn_w_gate=ffn_w_gate.astype(BF16), ffn_w_up=ffn_w_up.astype(BF16),
             ffn_w_down=ffn_w_down.astype(BF16))
    y_p, conv_p = _trunk_prompt(x_prompt, w)
    y_s, conv_s, v_s = _trunk_sample(x_sample, state_conv, w)
    return (y_p, y_s, conv_p, conv_s, v_s)
```

```python
import functools
import math

import jax
import jax.numpy as jnp
from jax import lax
from jax.experimental import pallas as pl
from jax.experimental.pallas import tpu as pltpu

D_MODEL = 1024
DEPTH = 4
SGU_CHUNK = 128
SGU_HEADS = 4
D_SGU = 3 * D_MODEL
SGU_HEAD_DIM = D_SGU // SGU_HEADS
CONV_WIDTH = 31
CONV_STATE = CONV_WIDTH - 1
D_FF = 2816
RMS_EPS = 1e-6
LN_EPS = 1e-5

LANES = 128
N_LANE_TILES = D_MODEL // LANES
HALO = 32
HALO_PAD = HALO - CONV_STATE
COL_CHUNK = 256
CONV_ROW_BLOCK = 64
VMEM_LIMIT_BYTES = 56 * 1024 * 1024

PROMPT_TILE = 512

F32 = jnp.float32
BF16 = jnp.bfloat16

_GELU_C = math.sqrt(2.0 / math.pi)


def _rms(x, g):
    return x * lax.rsqrt(jnp.mean(x * x, axis=-1, keepdims=True) + RMS_EPS) * g


def _gelu(z):
    w = (z * z) * (_GELU_C * 0.044715) + _GELU_C
    hz = 0.5 * z
    return hz + hz * jnp.tanh(z * w)


def _dot(a, b):
    return jnp.dot(a, b, preferred_element_type=F32)


def _lane_partial(v):
    acc = v[:, :LANES]
    for k in range(LANES, v.shape[1], LANES):
        acc = acc + v[:, k:k + LANES]
    return acc


def _layer_spec(layer, shape):
    nd = len(shape)
    return pl.BlockSpec((None,) + tuple(shape), lambda i: (layer,) + (0,) * nd, pipeline_mode=pl.Buffered(1))


def _const_spec(shape):
    nd = len(shape)
    return pl.BlockSpec(shape, lambda i: (0,) * nd, pipeline_mode=pl.Buffered(1))


def _tile_spec(tile, n_tiles, shift=0):
    return pl.BlockSpec((tile, D_MODEL), lambda i: (jnp.clip(i - shift, 0, n_tiles - 1), 0))


def _params():
    return pltpu.CompilerParams(dimension_semantics=("arbitrary",), vmem_limit_bytes=VMEM_LIMIT_BYTES)


def _ffn_chunk(h, wg_ref, wu_ref, act_ref, j):
    g = _dot(h, wg_ref[:, j:j + COL_CHUNK])
    u = _dot(h, wu_ref[:, j:j + COL_CHUNK])
    act_ref[0:h.shape[0], j:j + COL_CHUNK] = (jax.nn.silu(g) * u).astype(BF16)


def _ffn_rows(x_ref, o_ref, g_ref, wg_ref, wu_ref, wd_ref, gf_ref, act_ref):
    rows = x_ref.shape[0]
    x = x_ref[...]
    h = _rms(x, g_ref[...]).astype(BF16)
    for j in range(0, D_FF, COL_CHUNK):
        _ffn_chunk(h, wg_ref, wu_ref, act_ref, j)
    xo = x + _dot(act_ref[0:rows, :], wd_ref[...])
    if gf_ref is not None:
        xo = _rms(xo, gf_ref[...])
    o_ref[...] = xo


def _ffn_kernel(*refs, n_tiles, final):
    xp_ref, xs_ref, g_ref, wg_ref, wu_ref, wd_ref = refs[:6]
    gf_ref = refs[6] if final else None
    op_ref, os_ref, act_ref = refs[-3:]
    i = pl.program_id(0)

    @pl.when(i < n_tiles)
    def _():
        _ffn_rows(xp_ref, op_ref, g_ref, wg_ref, wu_ref, wd_ref, gf_ref, act_ref)

    @pl.when(i == n_tiles)
    def _():
        _ffn_rows(xs_ref, os_ref, g_ref, wg_ref, wu_ref, wd_ref, gf_ref, act_ref)


def _ffn_specs(layer):
    return [_layer_spec(layer, (1, D_MODEL)), _layer_spec(layer, (D_MODEL, D_FF)),
            _layer_spec(layer, (D_MODEL, D_FF)), _layer_spec(layer, (D_FF, D_MODEL))]


def _ffn_args(w, layer):
    final = layer == DEPTH - 1
    args = [w["norm_ffn_g"], w["ffn_w_gate"], w["ffn_w_up"], w["ffn_w_down"]]
    specs = _ffn_specs(layer)
    if final:
        args.append(w["norm_final_g"])
        specs.append(_const_spec((1, D_MODEL)))
    return args, specs


def _ffn(xp, xs, w, layer, tile):
    n_tiles = xp.shape[0] // tile
    final = layer == DEPTH - 1
    args, specs = _ffn_args(w, layer)
    sample_spec = _const_spec(xs.shape)
    return pl.pallas_call(
        functools.partial(_ffn_kernel, n_tiles=n_tiles, final=final),
        out_shape=(jax.ShapeDtypeStruct(xp.shape, F32), jax.ShapeDtypeStruct(xs.shape, F32)),
        grid=(n_tiles + 1,),
        in_specs=[_tile_spec(tile, n_tiles), sample_spec] + specs,
        out_specs=(_tile_spec(tile, n_tiles), sample_spec),
        scratch_shapes=[pltpu.VMEM((tile, D_FF), BF16)],
        compiler_params=_params(),
        name="ffn_final" if final else "ffn",
    )(xp, xs, *args)


def _sgu_rows(x_ref, o_ref, v_ref, g_ref, win_ref, bin_ref, lng_ref, lnb_ref, ws_ref, bsb_ref, wout_ref, bout_ref,
              u_scr, v_scr, vn_scr, gated_scr, *, chunk_len):
    tile = x_ref.shape[0]
    h = _rms(x_ref[...], g_ref[...]).astype(BF16)

    s1 = jnp.zeros((tile, LANES), F32)
    s2 = jnp.zeros((tile, LANES), F32)
    for j in range(0, D_SGU, COL_CHUNK):
        z = _dot(h, win_ref[:, D_SGU + j:D_SGU + j + COL_CHUNK]) + bin_ref[:, D_SGU + j:D_SGU + j + COL_CHUNK]
        v = _gelu(z)
        s1 = s1 + _lane_partial(v)
        s2 = s2 + _lane_partial(v * v)
        v_scr[0:tile, j:j + COL_CHUNK] = v
    mean = jnp.sum(s1, axis=-1, keepdims=True) * (1.0 / D_SGU)
    var = jnp.sum(s2, axis=-1, keepdims=True) * (1.0 / D_SGU) - mean * mean
    rstd = lax.rsqrt(var + LN_EPS)
    for j in range(0, D_SGU, COL_CHUNK):
        u_scr[0:tile, j:j + COL_CHUNK] = _dot(h, win_ref[:, j:j + COL_CHUNK]) + bin_ref[:, j:j + COL_CHUNK]
        vn = (v_scr[0:tile, j:j + COL_CHUNK] - mean) * rstd * lng_ref[:, j:j + COL_CHUNK] + lnb_ref[:, j:j + COL_CHUNK]
        if v_ref is not None:
            v_ref[:, j:j + COL_CHUNK] = vn
        vn_scr[0:tile, j:j + COL_CHUNK] = vn.astype(BF16)

    o_ref[...] = x_ref[...] + bout_ref[...]
    row = lax.broadcasted_iota(jnp.int32, (chunk_len, chunk_len), 0)
    col = lax.broadcasted_iota(jnp.int32, (chunk_len, chunk_len), 1)
    for hd in range(SGU_HEADS):
        w = jnp.where(row >= col, ws_ref[hd, 0:chunk_len, 0:chunk_len], 0.0).astype(BF16)
        bias = jnp.concatenate([bsb_ref[hd, 0:chunk_len, :]] * (SGU_HEAD_DIM // LANES), axis=1)
        c0 = hd * SGU_HEAD_DIM
        for r0 in range(0, tile, chunk_len):
            mixed = _dot(w, vn_scr[r0:r0 + chunk_len, c0:c0 + SGU_HEAD_DIM]) + bias
            gated_scr[r0:r0 + chunk_len, c0:c0 + SGU_HEAD_DIM] = (
                _gelu(u_scr[r0:r0 + chunk_len, c0:c0 + SGU_HEAD_DIM]) * mixed).astype(BF16)
        o_ref[...] += _dot(gated_scr[0:tile, c0:c0 + SGU_HEAD_DIM], wout_ref[c0:c0 + SGU_HEAD_DIM, :])


def _sgu_kernel(xp_ref, xs_ref, *refs, n_tiles, sample_chunk_len):
    params, (op_ref, os_ref, v_ref), scratch = refs[:9], refs[9:12], refs[12:]
    i = pl.program_id(0)

    @pl.when(i < n_tiles)
    def _():
        _sgu_rows(xp_ref, op_ref, None, *params, *scratch, chunk_len=SGU_CHUNK)

    @pl.when(i == n_tiles)
    def _():
        _sgu_rows(xs_ref, os_ref, v_ref, *params, *scratch, chunk_len=sample_chunk_len)


def _sgu(xp, xs, w, layer, tile, sample_chunk_len):
    n_tiles = xp.shape[0] // tile
    j = layer // 2
    in_specs = [_tile_spec(tile, n_tiles), _const_spec(xs.shape),
                _layer_spec(layer, (1, D_MODEL)), _layer_spec(j, (D_MODEL, 2 * D_SGU)),
                _layer_spec(j, (1, 2 * D_SGU)), _layer_spec(j, (1, D_SGU)), _layer_spec(j, (1, D_SGU)),
                _layer_spec(j, (SGU_HEADS, SGU_CHUNK, SGU_CHUNK)), _layer_spec(j, (SGU_HEADS, SGU_CHUNK, LANES)),
                _layer_spec(j, (D_SGU, D_MODEL)), _layer_spec(j, (1, D_MODEL))]
    v_shape = (xs.shape[0], D_SGU)
    bsb = jnp.broadcast_to(w["sgu_b_s"][:, :, :, None], w["sgu_b_s"].shape + (LANES,))
    return pl.pallas_call(
        functools.partial(_sgu_kernel, n_tiles=n_tiles, sample_chunk_len=sample_chunk_len),
        out_shape=(jax.ShapeDtypeStruct(xp.shape, F32), jax.ShapeDtypeStruct(xs.shape, F32),
                   jax.ShapeDtypeStruct(v_shape, F32)),
        grid=(n_tiles + 1,),
        in_specs=in_specs,
        out_specs=(_tile_spec(tile, n_tiles), _const_spec(xs.shape), _const_spec(v_shape)),
        scratch_shapes=[pltpu.VMEM((tile, D_SGU), F32), pltpu.VMEM((tile, D_SGU), F32),
                        pltpu.VMEM((tile, D_SGU), BF16), pltpu.VMEM((tile, D_SGU), BF16)],
        compiler_params=_params(),
        name="sgu",
    )(xp, xs, w["norm_mix_g"], w["sgu_w_in"], w["sgu_b_in"], w["sgu_ln_g"], w["sgu_ln_b"], w["sgu_w_s"], bsb,
      w["sgu_w_out"], w["sgu_b_out"])


def _conv_mixer(x, g_ref, w1_ref, b1_ref, wdw_ref, bdw_ref, lng_ref, lnb_ref, w2_ref, b2_ref, st_ref,
                pad_scr, conv_scr, *, nseg, seg_len, carry, between=None):
    row_block = min(seg_len, CONV_ROW_BLOCK)
    h = _rms(x, g_ref[...]).astype(BF16)
    for j in range(0, D_MODEL, COL_CHUNK):
        a1 = _dot(h, w1_ref[:, j:j + COL_CHUNK]) + b1_ref[:, j:j + COL_CHUNK]
        a2 = _dot(h, w1_ref[:, D_MODEL + j:D_MODEL + j + COL_CHUNK]) + b1_ref[:, D_MODEL + j:D_MODEL + j + COL_CHUNK]
        glu = a1 * jax.nn.sigmoid(a2)
        for s in range(nseg):
            for jj in range(0, COL_CHUNK, LANES):
                pad_scr[s, (j + jj) // LANES, HALO:HALO + seg_len, :] = glu[s * seg_len:(s + 1) * seg_len,
                                                                            jj:jj + LANES]
        for l in range(j // LANES, (j + COL_CHUNK) // LANES):
            for s in range(nseg):
                for r0 in range(0, seg_len, row_block):
                    acc = jnp.broadcast_to(bdw_ref[l], (row_block, LANES))
                    for k in range(CONV_WIDTH):
                        acc = acc + wdw_ref[l, k:k + 1, :] * pad_scr[
                            s, l, r0 + k + HALO_PAD:r0 + k + HALO_PAD + row_block, :]
                    conv_scr[l, s * seg_len + r0:s * seg_len + r0 + row_block, :] = acc
                    if between is not None:
                        between(l, s * seg_len + r0)

    for s in range(nseg):
        st_ref[s] = jnp.concatenate(
            [pad_scr[s, l, HALO + seg_len - CONV_STATE:HALO + seg_len, :] for l in range(N_LANE_TILES)], axis=1)
    if carry:
        pad_scr[:, :, 0:HALO, :] = pad_scr[:, :, seg_len:seg_len + HALO, :]

    c = jnp.concatenate([conv_scr[l, 0:nseg * seg_len, :] for l in range(N_LANE_TILES)], axis=1)
    mu = jnp.mean(c, axis=-1, keepdims=True)
    cc = c - mu
    var = jnp.mean(cc * cc, axis=-1, keepdims=True)
    cn = cc * lax.rsqrt(var + LN_EPS) * lng_ref[...] + lnb_ref[...]
    act = jax.nn.silu(cn).astype(BF16)
    return x + _dot(act, w2_ref[...]) + b2_ref[...]


def _conv_specs(layer):
    j = layer // 2
    return [_layer_spec(layer, (1, D_MODEL)), _layer_spec(j, (D_MODEL, 2 * D_MODEL)), _layer_spec(j, (1, 2 * D_MODEL)),
            _layer_spec(j, (N_LANE_TILES, CONV_WIDTH, LANES)), _layer_spec(j, (N_LANE_TILES, 1, LANES)),
            _layer_spec(j, (1, D_MODEL)), _layer_spec(j, (1, D_MODEL)), _layer_spec(j, (D_MODEL, D_MODEL)),
            _layer_spec(j, (1, D_MODEL))]


def _conv_args(w):
    return [w["norm_mix_g"], w["conv_w_pw1"], w["conv_b_pw1"], w["conv_w_dw"], w["conv_b_dw"],
            w["conv_ln_g"], w["conv_ln_b"], w["conv_w_pw2"], w["conv_b_pw2"]]


_FFN_SLOTS = ((0, 128), (0, 384), (1, 128), (2, 128), (2, 384), (3, 128), (4, 128), (4, 384), (5, 128),
              (6, 128), (7, 128))
assert len(_FFN_SLOTS) == D_FF // COL_CHUNK


def _conv_ffn_kernel(xp_ref, xs_ref, past_ref, *refs, n_tiles, tiles_per_seq, final):
    conv_params, ffn_params = refs[:9], refs[9:13]
    rest = refs[13:]
    gfin_ref = None
    if final:
        gfin_ref, rest = rest[0], rest[1:]
    op_ref, os_ref, stp_ref, sts_ref, pad_scr, pads_scr, conv_scr, xmid_scr, xprev_scr, act_scr = rest
    gf_ref, wg_ref, wu_ref, wd_ref = ffn_params
    i = pl.program_id(0)

    @pl.when(i == 0)
    def _():
        xmid_scr[...] = jnp.zeros(xmid_scr.shape, F32)

    @pl.when(jnp.minimum(i, n_tiles - 1) % tiles_per_seq == 0)
    def _():
        pad_scr[:, :, 0:HALO, :] = jnp.zeros((1, N_LANE_TILES, HALO, LANES), F32)

    @pl.when(i <= n_tiles)
    def _():
        xprev_scr[...] = xmid_scr[...]
        xp = xprev_scr[...]
        hf = _rms(xp, gf_ref[...]).astype(BF16)

        def between(l, r0):
            if (l, r0) in _FFN_SLOTS:
                _ffn_chunk(hf, wg_ref, wu_ref, act_scr, _FFN_SLOTS.index((l, r0)) * COL_CHUNK)

        xmid_scr[...] = _conv_mixer(xp_ref[...], *conv_params, stp_ref, pad_scr, conv_scr, nseg=1,
                                    seg_len=xp_ref.shape[0], carry=True, between=between)
        y = xp + _dot(act_scr[...], wd_ref[...])
        if final:
            y = _rms(y, gfin_ref[...])
        op_ref[...] = y

    @pl.when(i == n_tiles + 1)
    def _():
        nseg, seg_len = past_ref.shape[0], xs_ref.shape[0] // past_ref.shape[0]
        for l in range(N_LANE_TILES):
            pads_scr[:, l, HALO_PAD:HALO, :] = past_ref[:, :, l * LANES:(l + 1) * LANES]
        xs_mid = xmid_scr.at[pl.ds(0, xs_ref.shape[0])]
        xs_mid[...] = _conv_mixer(xs_ref[...], *conv_params, sts_ref, pads_scr, conv_scr, nseg=nseg,
                                  seg_len=seg_len, carry=False)
        _ffn_rows(xs_mid, os_ref, gf_ref, wg_ref, wu_ref, wd_ref, gfin_ref, act_scr)


def _conv_ffn(xp, xs, past, w, layer, tile, seq_len):
    rows = xp.shape[0]
    n_tiles = rows // tile
    tiles_per_seq = seq_len // tile
    final = layer == DEPTH - 1
    nseg = past.shape[0]
    seg_len = xs.shape[0] // nseg
    ffn_args, ffn_specs = _ffn_args(w, layer)
    stp_shape = (rows // seq_len, CONV_STATE, D_MODEL)
    return pl.pallas_call(
        functools.partial(_conv_ffn_kernel, n_tiles=n_tiles, tiles_per_seq=tiles_per_seq, final=final),
        out_shape=(jax.ShapeDtypeStruct(xp.shape, F32), jax.ShapeDtypeStruct(xs.shape, F32),
                   jax.ShapeDtypeStruct(stp_shape, F32), jax.ShapeDtypeStruct(past.shape, F32)),
        grid=(n_tiles + 2,),
        in_specs=[_tile_spec(tile, n_tiles), _const_spec(xs.shape), _const_spec(past.shape)]
        + _conv_specs(layer) + ffn_specs,
        out_specs=(_tile_spec(tile, n_tiles, shift=1), _const_spec(xs.shape),
                   pl.BlockSpec((1, CONV_STATE, D_MODEL),
                                lambda i: (jnp.minimum(i, n_tiles - 1) // tiles_per_seq, 0, 0)),
                   _const_spec(past.shape)),
        scratch_shapes=[pltpu.VMEM((1, N_LANE_TILES, HALO + tile, LANES), F32),
                        pltpu.VMEM((nseg, N_LANE_TILES, HALO + seg_len, LANES), F32),
                        pltpu.VMEM((N_LANE_TILES, tile, LANES), F32),
                        pltpu.VMEM((tile, D_MODEL), F32), pltpu.VMEM((tile, D_MODEL), F32),
                        pltpu.VMEM((tile, D_FF), BF16)],
        compiler_params=_params(),
        name="conv_ffn_final" if final else "conv_ffn",
    )(xp, xs, past, *_conv_args(w), *ffn_args)


def _lane_tile_major(a):
    return a.reshape(a.shape[0], a.shape[1], N_LANE_TILES, LANES).transpose(0, 2, 1, 3)


def _trunk(x_prompt, x_sample, state_conv, w):
    nb_p, t_p, _ = x_prompt.shape
    nb_s, t_s, _ = x_sample.shape
    xp = x_prompt.reshape(nb_p * t_p, D_MODEL)
    xs = x_sample.reshape(nb_s * t_s, D_MODEL)
    conv_p, conv_s, sgu_vs = [], [], []
    for layer in range(DEPTH):
        if layer % 2 == 0:
            xp, xs, v = _sgu(xp, xs, w, layer, PROMPT_TILE, t_s)
            sgu_vs.append(v.reshape(nb_s, t_s, D_SGU))
            xp, xs = _ffn(xp, xs, w, layer, PROMPT_TILE)
        else:
            xp, xs, st_p, st_s = _conv_ffn(xp, xs, state_conv[layer // 2], w, layer, PROMPT_TILE, t_p)
            conv_p.append(st_p)
            conv_s.append(st_s)
    return (xp.reshape(nb_p, t_p, D_MODEL), xs.reshape(nb_s, t_s, D_MODEL), jnp.stack(conv_p), jnp.stack(conv_s),
            jnp.stack(sgu_vs))


def kernel(x_prompt, x_sample, state_conv, norm_mix_g, norm_ffn_g, norm_final_g, sgu_w_in, sgu_b_in, sgu_ln_g, sgu_ln_b, sgu_w_s, sgu_b_s, sgu_w_out, sgu_b_out, conv_w_pw1, conv_b_pw1, conv_w_dw, conv_b_dw, conv_ln_g, conv_ln_b, conv_w_pw2, conv_b_pw2, ffn_w_gate, ffn_w_up, ffn_w_down):
    assert CONV_STATE <= x_sample.shape[1] <= SGU_CHUNK
    assert x_prompt.shape[1] % PROMPT_TILE == 0 and x_sample.shape[0] * x_sample.shape[1] <= PROMPT_TILE
    row = lambda a: a[:, None, :]
    w = dict(norm_mix_g=row(norm_mix_g), norm_ffn_g=row(norm_ffn_g), norm_final_g=norm_final_g[None],
             sgu_w_in=sgu_w_in.astype(BF16), sgu_b_in=row(sgu_b_in), sgu_ln_g=row(sgu_ln_g),
             sgu_ln_b=row(sgu_ln_b), sgu_w_s=sgu_w_s, sgu_b_s=sgu_b_s, sgu_w_out=sgu_w_out.astype(BF16),
             sgu_b_out=row(sgu_b_out),
             conv_w_pw1=conv_w_pw1.astype(BF16), conv_b_pw1=row(conv_b_pw1), conv_w_dw=_lane_tile_major(conv_w_dw),
             conv_b_dw=_lane_tile_major(row(conv_b_dw)), conv_ln_g=row(conv_ln_g), conv_ln_b=row(conv_ln_b),
             conv_w_pw2=conv_w_pw2.astype(BF16), conv_b_pw2=row(conv_b_pw2),
             ffn_w_gate=ffn_w_gate.astype(BF16), ffn_w_up=ffn_w_up.astype(BF16),
             ffn_w_down=ffn_w_down.astype(BF16))
    return _trunk(x_prompt, x_sample, state_conv, w)
```

```python
import functools
import math

import jax
import jax.numpy as jnp
from jax import lax
from jax.experimental import pallas as pl
from jax.experimental.pallas import tpu as pltpu

D_MODEL = 1024
DEPTH = 4
SGU_CHUNK = 128
SGU_HEADS = 4
D_SGU = 3 * D_MODEL
SGU_HEAD_DIM = D_SGU // SGU_HEADS
CONV_WIDTH = 31
CONV_STATE = CONV_WIDTH - 1
D_FF = 2816
RMS_EPS = 1e-6
LN_EPS = 1e-5

LANES = 128
N_LANE_TILES = D_MODEL // LANES
HALO = 32
HALO_PAD = HALO - CONV_STATE
COL_CHUNK = 256
CONV_ROW_BLOCK = 64
VMEM_LIMIT_BYTES = 56 * 1024 * 1024

PROMPT_TILE = 512

F32 = jnp.float32
BF16 = jnp.bfloat16

_GELU_C = math.sqrt(2.0 / math.pi)


def _rms(x, g):
    return x * lax.rsqrt(jnp.mean(x * x, axis=-1, keepdims=True) + RMS_EPS) * g


def _gelu(z):
    w = (z * z) * (_GELU_C * 0.044715) + _GELU_C
    hz = 0.5 * z
    return hz + hz * jnp.tanh(z * w)


def _dot(a, b):
    return jnp.dot(a, b, preferred_element_type=F32)


def _lane_partial(v):
    acc = v[:, :LANES]
    for k in range(LANES, v.shape[1], LANES):
        acc = acc + v[:, k:k + LANES]
    return acc


def _layer_norm_stats(s1, s2, n):
    mean = jnp.sum(s1, axis=-1, keepdims=True) * (1.0 / n)
    var = jnp.sum(s2, axis=-1, keepdims=True) * (1.0 / n) - mean * mean
    return mean, lax.rsqrt(var + LN_EPS)


def _causal(ws, chunk_len):
    row = lax.broadcasted_iota(jnp.int32, (chunk_len, chunk_len), 0)
    col = lax.broadcasted_iota(jnp.int32, (chunk_len, chunk_len), 1)
    return jnp.where(row >= col, ws, 0.0).astype(BF16)


def _layer_spec(layer, shape):
    nd = len(shape)
    return pl.BlockSpec((None,) + tuple(shape), lambda i: (layer,) + (0,) * nd, pipeline_mode=pl.Buffered(1))


def _const_spec(shape):
    nd = len(shape)
    return pl.BlockSpec(shape, lambda i: (0,) * nd, pipeline_mode=pl.Buffered(1))


def _params():
    return pltpu.CompilerParams(dimension_semantics=("arbitrary",), vmem_limit_bytes=VMEM_LIMIT_BYTES)


def _ffn_chunk(h, wg_ref, wu_ref, act_ref, j):
    g = _dot(h, wg_ref[:, j:j + COL_CHUNK])
    u = _dot(h, wu_ref[:, j:j + COL_CHUNK])
    act_ref[:, j:j + COL_CHUNK] = (jax.nn.silu(g) * u).astype(BF16)


def _ffn_kernel(*refs, final):
    if final:
        x_ref, g_ref, wg_ref, wu_ref, wd_ref, gf_ref, o_ref, act_ref = refs
    else:
        x_ref, g_ref, wg_ref, wu_ref, wd_ref, o_ref, act_ref = refs
    x = x_ref[...]
    h = _rms(x, g_ref[...]).astype(BF16)
    for j in range(0, D_FF, COL_CHUNK):
        _ffn_chunk(h, wg_ref, wu_ref, act_ref, j)
    xo = x + _dot(act_ref[...], wd_ref[...])
    if final:
        xo = _rms(xo, gf_ref[...])
    o_ref[...] = xo


def _ffn_specs(layer):
    specs = [_layer_spec(layer, (1, D_MODEL)), _const_spec((D_MODEL, D_FF)), _const_spec((D_MODEL, D_FF)),
             _const_spec((D_FF, D_MODEL))]
    if layer == DEPTH - 1:
        specs.append(_const_spec((1, D_MODEL)))
    return specs


def _ffn_args(w, wb, layer):
    args = [w["norm_ffn_g"], wb["gate"], wb["up"], wb["down"]]
    if layer == DEPTH - 1:
        args.append(w["norm_final_g"])
    return args


def _ffn(x, w, wb, layer, tile):
    rows = x.shape[0]
    final = layer == DEPTH - 1
    row_spec = pl.BlockSpec((tile, D_MODEL), lambda i: (i, 0))
    return pl.pallas_call(
        functools.partial(_ffn_kernel, final=final),
        out_shape=jax.ShapeDtypeStruct((rows, D_MODEL), F32),
        grid=(rows // tile,),
        in_specs=[row_spec] + _ffn_specs(layer),
        out_specs=row_spec,
        scratch_shapes=[pltpu.VMEM((tile, D_FF), BF16)],
        compiler_params=_params(),
        name="ffn_final" if final else "ffn",
    )(x, *_ffn_args(w, wb, layer))


def _ffn_sample_kernel(*refs, n_chunks, final):
    x_ref, g_ref, wg_ref, wu_ref, wd_ref = refs[:5]
    gf_ref = refs[5] if final else None
    o_ref, wgo_ref, wuo_ref, wdo_ref, h_scr, acc_scr = refs[-6:]
    j = pl.program_id(0)

    @pl.when(j == 0)
    def _():
        x = x_ref[...]
        h_scr[...] = _rms(x, g_ref[...]).astype(BF16)
        acc_scr[...] = x

    wg = wg_ref[...].astype(BF16)
    wu = wu_ref[...].astype(BF16)
    wd = wd_ref[...].astype(BF16)
    wgo_ref[...] = wg
    wuo_ref[...] = wu
    wdo_ref[...] = wd
    h = h_scr[...]
    act = (jax.nn.silu(_dot(h, wg)) * _dot(h, wu)).astype(BF16)
    acc_scr[...] += _dot(act, wd)

    @pl.when(j == n_chunks - 1)
    def _():
        y = acc_scr[...]
        if final:
            y = _rms(y, gf_ref[...])
        o_ref[...] = y


def _ffn_sample(x, w, layer):
    rows = x.shape[0]
    final = layer == DEPTH - 1
    n_chunks = D_FF // COL_CHUNK
    col_in = pl.BlockSpec((None, D_MODEL, COL_CHUNK), lambda j: (layer, 0, j))
    row_in = pl.BlockSpec((None, COL_CHUNK, D_MODEL), lambda j: (layer, j, 0))
    col_out = pl.BlockSpec((D_MODEL, COL_CHUNK), lambda j: (0, j))
    row_out = pl.BlockSpec((COL_CHUNK, D_MODEL), lambda j: (j, 0))
    in_specs = [_const_spec((rows, D_MODEL)), _layer_spec(layer, (1, D_MODEL)), col_in, col_in, row_in]
    args = [x, w["norm_ffn_g"], w["ffn_w_gate"], w["ffn_w_up"], w["ffn_w_down"]]
    if final:
        in_specs.append(_const_spec((1, D_MODEL)))
        args.append(w["norm_final_g"])
    y, gate, up, down = pl.pallas_call(
        functools.partial(_ffn_sample_kernel, n_chunks=n_chunks, final=final),
        out_shape=(jax.ShapeDtypeStruct((rows, D_MODEL), F32), jax.ShapeDtypeStruct((D_MODEL, D_FF), BF16),
                   jax.ShapeDtypeStruct((D_MODEL, D_FF), BF16), jax.ShapeDtypeStruct((D_FF, D_MODEL), BF16)),
        grid=(n_chunks,),
        in_specs=in_specs,
        out_specs=(_const_spec((rows, D_MODEL)), col_out, col_out, row_out),
        scratch_shapes=[pltpu.VMEM((rows, D_MODEL), BF16), pltpu.VMEM((rows, D_MODEL), F32)],
        compiler_params=_params(),
        name="ffn_sample_final" if final else "ffn_sample",
    )(*args)
    return y, dict(gate=gate, up=up, down=down)


def _sgu_kernel(x_ref, g_ref, win_ref, bin_ref, lng_ref, lnb_ref, ws_ref, bsb_ref, wout_ref, bout_ref,
                o_ref, u_scr, v_scr, vn_scr, gated_scr):
    tile = x_ref.shape[0]
    h = _rms(x_ref[...], g_ref[...]).astype(BF16)

    s1 = jnp.zeros((tile, LANES), F32)
    s2 = jnp.zeros((tile, LANES), F32)
    for j in range(0, D_SGU, COL_CHUNK):
        z = _dot(h, win_ref[:, D_SGU + j:D_SGU + j + COL_CHUNK]) + bin_ref[:, D_SGU + j:D_SGU + j + COL_CHUNK]
        v = _gelu(z)
        s1 = s1 + _lane_partial(v)
        s2 = s2 + _lane_partial(v * v)
        v_scr[:, j:j + COL_CHUNK] = v
    mean, rstd = _layer_norm_stats(s1, s2, D_SGU)
    for j in range(0, D_SGU, COL_CHUNK):
        u_scr[:, j:j + COL_CHUNK] = _dot(h, win_ref[:, j:j + COL_CHUNK]) + bin_ref[:, j:j + COL_CHUNK]
        vn = (v_scr[:, j:j + COL_CHUNK] - mean) * rstd * lng_ref[:, j:j + COL_CHUNK] + lnb_ref[:, j:j + COL_CHUNK]
        vn_scr[:, j:j + COL_CHUNK] = vn.astype(BF16)

    o_ref[...] = x_ref[...] + bout_ref[...]
    for hd in range(SGU_HEADS):
        w = _causal(ws_ref[hd], SGU_CHUNK)
        bias = jnp.concatenate([bsb_ref[hd]] * (SGU_HEAD_DIM // LANES), axis=1)
        c0 = hd * SGU_HEAD_DIM
        for r0 in range(0, tile, SGU_CHUNK):
            mixed = _dot(w, vn_scr[r0:r0 + SGU_CHUNK, c0:c0 + SGU_HEAD_DIM]) + bias
            gated_scr[r0:r0 + SGU_CHUNK, c0:c0 + SGU_HEAD_DIM] = (
                _gelu(u_scr[r0:r0 + SGU_CHUNK, c0:c0 + SGU_HEAD_DIM]) * mixed).astype(BF16)
        o_ref[...] += _dot(gated_scr[:, c0:c0 + SGU_HEAD_DIM], wout_ref[c0:c0 + SGU_HEAD_DIM, :])


def _sgu(x, w, wb, layer, tile):
    rows = x.shape[0]
    j = layer // 2
    row_spec = pl.BlockSpec((tile, D_MODEL), lambda i: (i, 0))
    in_specs = [row_spec, _layer_spec(layer, (1, D_MODEL)), _const_spec((D_MODEL, 2 * D_SGU)),
                _layer_spec(j, (1, 2 * D_SGU)), _layer_spec(j, (1, D_SGU)), _layer_spec(j, (1, D_SGU)),
                _layer_spec(j, (SGU_HEADS, SGU_CHUNK, SGU_CHUNK)), _layer_spec(j, (SGU_HEADS, SGU_CHUNK, LANES)),
                _const_spec((D_SGU, D_MODEL)), _layer_spec(j, (1, D_MODEL))]
    bsb = jnp.broadcast_to(w["sgu_b_s"][:, :, :, None], w["sgu_b_s"].shape + (LANES,))
    return pl.pallas_call(
        _sgu_kernel,
        out_shape=jax.ShapeDtypeStruct((rows, D_MODEL), F32),
        grid=(rows // tile,),
        in_specs=in_specs,
        out_specs=row_spec,
        scratch_shapes=[pltpu.VMEM((tile, D_SGU), F32), pltpu.VMEM((tile, D_SGU), F32),
                        pltpu.VMEM((tile, D_SGU), BF16), pltpu.VMEM((tile, D_SGU), BF16)],
        compiler_params=_params(),
        name="sgu",
    )(x, w["norm_mix_g"], wb["w_in"], w["sgu_b_in"], w["sgu_ln_g"], w["sgu_ln_b"], w["sgu_w_s"], bsb,
      wb["w_out"], w["sgu_b_out"])


def _sgu_sample_kernel(x_ref, g_ref, win_ref, bin_ref, lng_ref, lnb_ref, ws_ref, bsb_ref, wout_ref, bout_ref,
                       o_ref, v_ref, wino_ref, wouto_ref,
                       h_scr, u_scr, v_scr, s1_scr, s2_scr, gated_scr, acc_scr, *, chunk_len):
    s = pl.program_id(0)
    rows = x_ref.shape[0]
    nh = SGU_HEADS

    @pl.when(s == 0)
    def _():
        h_scr[...] = _rms(x_ref[...], g_ref[...]).astype(BF16)
        s1_scr[...] = jnp.zeros(s1_scr.shape, F32)
        s2_scr[...] = jnp.zeros(s2_scr.shape, F32)

    @pl.when(s < 2 * nh)
    def _():
        wb = win_ref[...].astype(BF16)
        wino_ref[...] = wb
        z = _dot(h_scr[...], wb) + bin_ref[...]

        @pl.when(s < nh)
        def _():
            v = _gelu(z)
            s1_scr[...] += _lane_partial(v)
            s2_scr[...] += _lane_partial(v * v)
            v_scr[s] = v

        @pl.when(s >= nh)
        def _():
            u_scr[s - nh] = z

    @pl.when(s == 2 * nh)
    def _():
        mean, rstd = _layer_norm_stats(s1_scr[...], s2_scr[...], D_SGU)
        for hd in range(nh):
            c0 = hd * SGU_HEAD_DIM
            vn = (v_scr[hd] - mean) * rstd * lng_ref[:, c0:c0 + SGU_HEAD_DIM] + lnb_ref[:, c0:c0 + SGU_HEAD_DIM]
            v_ref[:, c0:c0 + SGU_HEAD_DIM] = vn
            vnb = vn.astype(BF16)
            w = _causal(ws_ref[hd], chunk_len)
            bias = jnp.concatenate([bsb_ref[hd]] * (SGU_HEAD_DIM // LANES), axis=1)
            for r0 in range(0, rows, chunk_len):
                mixed = _dot(w, vnb[r0:r0 + chunk_len]) + bias
                gated_scr[hd, r0:r0 + chunk_len, :] = (_gelu(u_scr[hd, r0:r0 + chunk_len, :]) * mixed).astype(BF16)
        acc_scr[...] = x_ref[...] + bout_ref[...]

    @pl.when(s >= 2 * nh)
    def _():
        wb = wout_ref[...].astype(BF16)
        wouto_ref[...] = wb
        acc_scr[...] += _dot(gated_scr[s - 2 * nh], wb)

    @pl.when(s == 3 * nh - 1)
    def _():
        o_ref[...] = acc_scr[...]


def _sgu_sample(x, w, layer, chunk_len):
    rows = x.shape[0]
    j = layer // 2
    nh = SGU_HEADS
    hw = SGU_HEAD_DIM
    in_col = lambda s: (jnp.minimum(s, 2 * nh - 1) + nh) % (2 * nh)
    out_row = lambda s: jnp.maximum(s - 2 * nh, 0)
    ws = w["sgu_w_s"][:, :, :chunk_len, :chunk_len]
    bsb = jnp.broadcast_to(w["sgu_b_s"][:, :, :chunk_len, None], ws.shape[:3] + (LANES,))
    in_specs = [_const_spec((rows, D_MODEL)), _layer_spec(layer, (1, D_MODEL)),
                pl.BlockSpec((None, D_MODEL, hw), lambda s: (j, 0, in_col(s))),
                pl.BlockSpec((None, 1, hw), lambda s: (j, 0, in_col(s))),
                _layer_spec(j, (1, D_SGU)), _layer_spec(j, (1, D_SGU)),
                _layer_spec(j, (nh, chunk_len, chunk_len)), _layer_spec(j, (nh, chunk_len, LANES)),
                pl.BlockSpec((None, hw, D_MODEL), lambda s: (j, out_row(s), 0)),
                _layer_spec(j, (1, D_MODEL))]
    y, v, w_in, w_out = pl.pallas_call(
        functools.partial(_sgu_sample_kernel, chunk_len=chunk_len),
        out_shape=(jax.ShapeDtypeStruct((rows, D_MODEL), F32), jax.ShapeDtypeStruct((rows, D_SGU), F32),
                   jax.ShapeDtypeStruct((D_MODEL, 2 * D_SGU), BF16), jax.ShapeDtypeStruct((D_SGU, D_MODEL), BF16)),
        grid=(3 * nh,),
        in_specs=in_specs,
        out_specs=(_const_spec((rows, D_MODEL)), _const_spec((rows, D_SGU)),
                   pl.BlockSpec((D_MODEL, hw), lambda s: (0, in_col(s))),
                   pl.BlockSpec((hw, D_MODEL), lambda s: (out_row(s), 0))),
        scratch_shapes=[pltpu.VMEM((rows, D_MODEL), BF16), pltpu.VMEM((nh, rows, hw), F32),
                        pltpu.VMEM((nh, rows, hw), F32), pltpu.VMEM((rows, LANES), F32),
                        pltpu.VMEM((rows, LANES), F32), pltpu.VMEM((nh, rows, hw), BF16),
                        pltpu.VMEM((rows, D_MODEL), F32)],
        compiler_params=_params(),
        name="sgu_sample",
    )(x, w["norm_mix_g"], w["sgu_w_in"], w["sgu_b_in"], w["sgu_ln_g"], w["sgu_ln_b"], ws, bsb,
      w["sgu_w_out"], w["sgu_b_out"])
    return y, v, dict(w_in=w_in, w_out=w_out)


def _depthwise(pad_scr, conv_scr, wdw_ref, bdw_ref, s, l, r0, out_row, row_block):
    acc = jnp.broadcast_to(bdw_ref[l], (row_block, LANES))
    for k in range(CONV_WIDTH):
        acc = acc + wdw_ref[l, k:k + 1, :] * pad_scr[s, l, r0 + k + HALO_PAD:r0 + k + HALO_PAD + row_block, :]
    conv_scr[l, out_row:out_row + row_block, :] = acc


def _conv_norm_act(conv_scr, lng_ref, lnb_ref):
    c = jnp.concatenate([conv_scr[l] for l in range(N_LANE_TILES)], axis=1)
    mu = jnp.mean(c, axis=-1, keepdims=True)
    cc = c - mu
    var = jnp.mean(cc * cc, axis=-1, keepdims=True)
    cn = cc * lax.rsqrt(var + LN_EPS) * lng_ref[...] + lnb_ref[...]
    return jax.nn.silu(cn).astype(BF16)


def _conv_state(pad_scr, s, seg_len):
    return jnp.concatenate(
        [pad_scr[s, l, HALO + seg_len - CONV_STATE:HALO + seg_len, :] for l in range(N_LANE_TILES)], axis=1)


def _conv_mixer(x, g_ref, w1a_ref, w1b_ref, b1_ref, wdw_ref, bdw_ref, lng_ref, lnb_ref, w2_ref, b2_ref, st_ref,
                pad_scr, conv_scr, *, between):
    seg_len = x.shape[0]
    h = _rms(x, g_ref[...]).astype(BF16)
    for j in range(0, D_MODEL, COL_CHUNK):
        a1 = _dot(h, w1a_ref[:, j:j + COL_CHUNK]) + b1_ref[:, j:j + COL_CHUNK]
        a2 = _dot(h, w1b_ref[:, j:j + COL_CHUNK]) + b1_ref[:, D_MODEL + j:D_MODEL + j + COL_CHUNK]
        glu = a1 * jax.nn.sigmoid(a2)
        for jj in range(0, COL_CHUNK, LANES):
            pad_scr[0, (j + jj) // LANES, HALO:HALO + seg_len, :] = glu[:, jj:jj + LANES]
        for l in range(j // LANES, (j + COL_CHUNK) // LANES):
            for r0 in range(0, seg_len, CONV_ROW_BLOCK):
                _depthwise(pad_scr, conv_scr, wdw_ref, bdw_ref, 0, l, r0, r0, CONV_ROW_BLOCK)
                between(l, r0)

    st_ref[0] = _conv_state(pad_scr, 0, seg_len)
    pad_scr[:, :, 0:HALO, :] = pad_scr[:, :, seg_len:seg_len + HALO, :]
    act = _conv_norm_act(conv_scr, lng_ref, lnb_ref)
    return x + _dot(act, w2_ref[...]) + b2_ref[...]


def _conv_sample_kernel(x_ref, g_ref, w1a_ref, w1b_ref, b1a_ref, b1b_ref, past_ref, wdw_ref, bdw_ref, lng_ref, lnb_ref,
                        w2_ref, b2_ref, o_ref, st_ref, w1ao_ref, w1bo_ref, w2o_ref,
                        h_scr, pad_scr, conv_scr, act_scr, acc_scr, *, nseg, seg_len):
    s = pl.program_id(0)
    n = D_MODEL // COL_CHUNK

    @pl.when(s == 0)
    def _():
        h_scr[...] = _rms(x_ref[...], g_ref[...]).astype(BF16)
        for l in range(N_LANE_TILES):
            pad_scr[:, l, HALO_PAD:HALO, :] = past_ref[:, :, l * LANES:(l + 1) * LANES]

    @pl.when(s < n)
    def _():
        wa = w1a_ref[...].astype(BF16)
        wb = w1b_ref[...].astype(BF16)
        w1ao_ref[...] = wa
        w1bo_ref[...] = wb
        h = h_scr[...]
        glu = (_dot(h, wa) + b1a_ref[...]) * jax.nn.sigmoid(_dot(h, wb) + b1b_ref[...])
        for jj in range(COL_CHUNK // LANES):
            l = s * (COL_CHUNK // LANES) + jj
            for q in range(nseg):
                pad_scr[q, l, HALO:HALO + seg_len, :] = glu[q * seg_len:(q + 1) * seg_len, jj * LANES:(jj + 1) * LANES]
            for q in range(nseg):
                _depthwise(pad_scr, conv_scr, wdw_ref, bdw_ref, q, l, 0, q * seg_len, seg_len)

    @pl.when(s == n)
    def _():
        for q in range(nseg):
            st_ref[q] = _conv_state(pad_scr, q, seg_len)
        act = _conv_norm_act(conv_scr, lng_ref, lnb_ref)
        for r in range(n):
            act_scr[r] = act[:, r * COL_CHUNK:(r + 1) * COL_CHUNK]
        acc_scr[...] = x_ref[...] + b2_ref[...]

    @pl.when(s >= n)
    def _():
        wb = w2_ref[...].astype(BF16)
        w2o_ref[...] = wb
        acc_scr[...] += _dot(act_scr[s - n], wb)

    @pl.when(s == 2 * n - 1)
    def _():
        o_ref[...] = acc_scr[...]


def _conv_sample(x, past, w, layer):
    rows = x.shape[0]
    nseg = past.shape[0]
    seg_len = rows // nseg
    j = layer // 2
    n = D_MODEL // COL_CHUNK
    col = lambda s: jnp.minimum(s, n - 1)
    krow = lambda s: jnp.maximum(s - n, 0)
    in_specs = [_const_spec((rows, D_MODEL)), _layer_spec(layer, (1, D_MODEL)),
                pl.BlockSpec((None, D_MODEL, COL_CHUNK), lambda s: (j, 0, col(s))),
                pl.BlockSpec((None, D_MODEL, COL_CHUNK), lambda s: (j, 0, n + col(s))),
                pl.BlockSpec((None, 1, COL_CHUNK), lambda s: (j, 0, col(s))),
                pl.BlockSpec((None, 1, COL_CHUNK), lambda s: (j, 0, n + col(s))),
                _const_spec((nseg, CONV_STATE, D_MODEL)),
                _layer_spec(j, (N_LANE_TILES, CONV_WIDTH, LANES)), _layer_spec(j, (N_LANE_TILES, 1, LANES)),
                _layer_spec(j, (1, D_MODEL)), _layer_spec(j, (1, D_MODEL)),
                pl.BlockSpec((None, COL_CHUNK, D_MODEL), lambda s: (j, krow(s), 0)),
                _layer_spec(j, (1, D_MODEL))]
    y, st, w1a, w1b, w2 = pl.pallas_call(
        functools.partial(_conv_sample_kernel, nseg=nseg, seg_len=seg_len),
        out_shape=(jax.ShapeDtypeStruct((rows, D_MODEL), F32), jax.ShapeDtypeStruct(past.shape, F32),
                   jax.ShapeDtypeStruct((D_MODEL, D_MODEL), BF16), jax.ShapeDtypeStruct((D_MODEL, D_MODEL), BF16),
                   jax.ShapeDtypeStruct((D_MODEL, D_MODEL), BF16)),
        grid=(2 * n,),
        in_specs=in_specs,
        out_specs=(_const_spec((rows, D_MODEL)), _const_spec(past.shape),
                   pl.BlockSpec((D_MODEL, COL_CHUNK), lambda s: (0, col(s))),
                   pl.BlockSpec((D_MODEL, COL_CHUNK), lambda s: (0, col(s))),
                   pl.BlockSpec((COL_CHUNK, D_MODEL), lambda s: (krow(s), 0))),
        scratch_shapes=[pltpu.VMEM((rows, D_MODEL), BF16),
                        pltpu.VMEM((nseg, N_LANE_TILES, HALO + seg_len, LANES), F32),
                        pltpu.VMEM((N_LANE_TILES, rows, LANES), F32),
                        pltpu.VMEM((n, rows, COL_CHUNK), BF16), pltpu.VMEM((rows, D_MODEL), F32)],
        compiler_params=_params(),
        name="conv_sample",
    )(x, w["norm_mix_g"], w["conv_w_pw1"], w["conv_w_pw1"], w["conv_b_pw1"], w["conv_b_pw1"], past,
      w["conv_w_dw"], w["conv_b_dw"], w["conv_ln_g"], w["conv_ln_b"], w["conv_w_pw2"], w["conv_b_pw2"])
    return y, st, dict(w1a=w1a, w1b=w1b, w2=w2)


_FFN_SLOTS = ((0, 128), (0, 384), (1, 128), (2, 128), (2, 384), (3, 128), (4, 128), (4, 384), (5, 128),
              (6, 128), (7, 128))
assert len(_FFN_SLOTS) == D_FF // COL_CHUNK


def _conv_ffn_kernel(*refs, n_tiles, tiles_per_seq, final):
    x_ref = refs[0]
    conv_params, (gf_ref, wg_ref, wu_ref, wd_ref) = refs[1:11], refs[11:15]
    rest = refs[15:]
    if final:
        gfin_ref, rest = rest[0], rest[1:]
    o_ref, st_ref, pad_scr, conv_scr, xmid_scr, xprev_scr, act_scr = rest
    i = pl.program_id(0)

    @pl.when(i == 0)
    def _():
        xmid_scr[...] = jnp.zeros(xmid_scr.shape, F32)

    xprev_scr[...] = xmid_scr[...]

    @pl.when(jnp.minimum(i, n_tiles - 1) % tiles_per_seq == 0)
    def _():
        pad_scr[:, :, 0:HALO, :] = jnp.zeros((1, N_LANE_TILES, HALO, LANES), F32)

    xp = xprev_scr[...]
    hf = _rms(xp, gf_ref[...]).astype(BF16)

    def between(l, r0):
        if (l, r0) in _FFN_SLOTS:
            _ffn_chunk(hf, wg_ref, wu_ref, act_scr, _FFN_SLOTS.index((l, r0)) * COL_CHUNK)

    xmid_scr[...] = _conv_mixer(x_ref[...], *conv_params, st_ref, pad_scr, conv_scr, between=between)
    y = xp + _dot(act_scr[...], wd_ref[...])
    if final:
        y = _rms(y, gfin_ref[...])
    o_ref[...] = y


def _conv_ffn(x, w, wb_conv, wb_ffn, layer, tile, seq_len):
    rows = x.shape[0]
    n_tiles = rows // tile
    tiles_per_seq = seq_len // tile
    final = layer == DEPTH - 1
    j = layer // 2
    sq = (D_MODEL, D_MODEL)
    conv_specs = [_layer_spec(layer, (1, D_MODEL)), _const_spec(sq), _const_spec(sq), _layer_spec(j, (1, 2 * D_MODEL)),
                  _layer_spec(j, (N_LANE_TILES, CONV_WIDTH, LANES)), _layer_spec(j, (N_LANE_TILES, 1, LANES)),
                  _layer_spec(j, (1, D_MODEL)), _layer_spec(j, (1, D_MODEL)), _const_spec(sq),
                  _layer_spec(j, (1, D_MODEL))]
    conv_args = [w["norm_mix_g"], wb_conv["w1a"], wb_conv["w1b"], w["conv_b_pw1"], w["conv_w_dw"], w["conv_b_dw"],
                 w["conv_ln_g"], w["conv_ln_b"], wb_conv["w2"], w["conv_b_pw2"]]
    return pl.pallas_call(
        functools.partial(_conv_ffn_kernel, n_tiles=n_tiles, tiles_per_seq=tiles_per_seq, final=final),
        out_shape=(jax.ShapeDtypeStruct((rows, D_MODEL), F32),
                   jax.ShapeDtypeStruct((rows // seq_len, CONV_STATE, D_MODEL), F32)),
        grid=(n_tiles + 1,),
        in_specs=[pl.BlockSpec((tile, D_MODEL), lambda i: (jnp.minimum(i, n_tiles - 1), 0))]
        + conv_specs + _ffn_specs(layer),
        out_specs=(pl.BlockSpec((tile, D_MODEL), lambda i: (jnp.maximum(i - 1, 0), 0)),
                   pl.BlockSpec((1, CONV_STATE, D_MODEL),
                                lambda i: (jnp.minimum(i, n_tiles - 1) // tiles_per_seq, 0, 0))),
        scratch_shapes=[pltpu.VMEM((1, N_LANE_TILES, HALO + tile, LANES), F32),
                        pltpu.VMEM((N_LANE_TILES, tile, LANES), F32),
                        pltpu.VMEM((tile, D_MODEL), F32), pltpu.VMEM((tile, D_MODEL), F32),
                        pltpu.VMEM((tile, D_FF), BF16)],
        compiler_params=_params(),
        name="conv_ffn_final" if final else "conv_ffn",
    )(x, *conv_args, *_ffn_args(w, wb_ffn, layer))


def _lane_tile_major(a):
    return a.reshape(a.shape[0], a.shape[1], N_LANE_TILES, LANES).transpose(0, 2, 1, 3)


def _trunk(x_prompt, x_sample, state_conv, w):
    nb_p, t_p, _ = x_prompt.shape
    nb_s, t_s, _ = x_sample.shape
    xp = x_prompt.reshape(nb_p * t_p, D_MODEL)
    xs = x_sample.reshape(nb_s * t_s, D_MODEL)
    conv_p, conv_s, sgu_vs = [], [], []
    for layer in range(DEPTH):
        if layer % 2 == 0:
            xs, v, wb_mix = _sgu_sample(xs, w, layer, t_s)
            sgu_vs.append(v.reshape(nb_s, t_s, D_SGU))
            xs, wb_ffn = _ffn_sample(xs, w, layer)
            xp = _sgu(xp, w, wb_mix, layer, PROMPT_TILE)
            xp = _ffn(xp, w, wb_ffn, layer, PROMPT_TILE)
        else:
            xs, st_s, wb_mix = _conv_sample(xs, state_conv[layer // 2], w, layer)
            xs, wb_ffn = _ffn_sample(xs, w, layer)
            xp, st_p = _conv_ffn(xp, w, wb_mix, wb_ffn, layer, PROMPT_TILE, t_p)
            conv_p.append(st_p)
            conv_s.append(st_s)
    return (xp.reshape(nb_p, t_p, D_MODEL), xs.reshape(nb_s, t_s, D_MODEL), jnp.stack(conv_p), jnp.stack(conv_s),
            jnp.stack(sgu_vs))


def kernel(x_prompt, x_sample, state_conv, norm_mix_g, norm_ffn_g, norm_final_g, sgu_w_in, sgu_b_in, sgu_ln_g, sgu_ln_b, sgu_w_s, sgu_b_s, sgu_w_out, sgu_b_out, conv_w_pw1, conv_b_pw1, conv_w_dw, conv_b_dw, conv_ln_g, conv_ln_b, conv_w_pw2, conv_b_pw2, ffn_w_gate, ffn_w_up, ffn_w_down):
    assert CONV_STATE <= x_sample.shape[1] <= SGU_CHUNK and x_prompt.shape[1] % PROMPT_TILE == 0
    row = lambda a: a[:, None, :]
    w = dict(norm_mix_g=row(norm_mix_g), norm_ffn_g=row(norm_ffn_g), norm_final_g=norm_final_g[None],
             sgu_w_in=sgu_w_in, sgu_b_in=row(sgu_b_in), sgu_ln_g=row(sgu_ln_g), sgu_ln_b=row(sgu_ln_b),
             sgu_w_s=sgu_w_s, sgu_b_s=sgu_b_s, sgu_w_out=sgu_w_out, sgu_b_out=row(sgu_b_out),
             conv_w_pw1=conv_w_pw1, conv_b_pw1=row(conv_b_pw1), conv_w_dw=_lane_tile_major(conv_w_dw),
             conv_b_dw=_lane_tile_major(row(conv_b_dw)), conv_ln_g=row(conv_ln_g), conv_ln_b=row(conv_ln_b),
             conv_w_pw2=conv_w_pw2, conv_b_pw2=row(conv_b_pw2),
             ffn_w_gate=ffn_w_gate, ffn_w_up=ffn_w_up, ffn_w_down=ffn_w_down)
    return _trunk(x_prompt, x_sample, state_conv, w)
```

```python
import functools
import math

import jax
import jax.numpy as jnp
from jax import lax
from jax.experimental import pallas as pl
from jax.experimental.pallas import tpu as pltpu

D_MODEL = 1024
DEPTH = 4
SGU_CHUNK = 128
SGU_HEADS = 4
D_SGU = 3 * D_MODEL
SGU_HEAD_DIM = D_SGU // SGU_HEADS
CONV_WIDTH = 31
CONV_STATE = CONV_WIDTH - 1
D_FF = 2816
RMS_EPS = 1e-6
LN_EPS = 1e-5

LANES = 128
N_LANE_TILES = D_MODEL // LANES
HALO = 32
HALO_PAD = HALO - CONV_STATE
COL_CHUNK = 256
CONV_ROW_BLOCK = 64
VMEM_LIMIT_BYTES = 56 * 1024 * 1024

PROMPT_TILE = 512
PROMPT_TILE_FFN = 1024

F32 = jnp.float32
BF16 = jnp.bfloat16

_GELU_C = math.sqrt(2.0 / math.pi)


def _rms(x, g):
    return x * lax.rsqrt(jnp.mean(x * x, axis=-1, keepdims=True) + RMS_EPS) * g


def _gelu(z):
    w = (z * z) * (_GELU_C * 0.044715) + _GELU_C
    hz = 0.5 * z
    return hz + hz * jnp.tanh(z * w)


def _dot(a, b):
    return jnp.dot(a, b, preferred_element_type=F32)


def _lane_partial(v):
    acc = v[:, :LANES]
    for k in range(LANES, v.shape[1], LANES):
        acc = acc + v[:, k:k + LANES]
    return acc


def _layer_norm_stats(s1, s2, n):
    mean = jnp.sum(s1, axis=-1, keepdims=True) * (1.0 / n)
    var = jnp.sum(s2, axis=-1, keepdims=True) * (1.0 / n) - mean * mean
    return mean, lax.rsqrt(var + LN_EPS)


def _causal(ws, chunk_len):
    row = lax.broadcasted_iota(jnp.int32, (chunk_len, chunk_len), 0)
    col = lax.broadcasted_iota(jnp.int32, (chunk_len, chunk_len), 1)
    return jnp.where(row >= col, ws, 0.0).astype(BF16)


def _layer_spec(layer, shape):
    nd = len(shape)
    return pl.BlockSpec((None,) + tuple(shape), lambda i: (layer,) + (0,) * nd, pipeline_mode=pl.Buffered(1))


def _const_spec(shape):
    nd = len(shape)
    return pl.BlockSpec(shape, lambda i: (0,) * nd, pipeline_mode=pl.Buffered(1))


def _params():
    return pltpu.CompilerParams(dimension_semantics=("arbitrary",), vmem_limit_bytes=VMEM_LIMIT_BYTES)


def _ffn_chunk(h, wg_ref, wu_ref, act_ref, j):
    g = _dot(h, wg_ref[:, j:j + COL_CHUNK])
    u = _dot(h, wu_ref[:, j:j + COL_CHUNK])
    act_ref[:, j:j + COL_CHUNK] = (jax.nn.silu(g) * u).astype(BF16)


def _ffn_kernel(*refs, final):
    if final:
        x_ref, g_ref, wg_ref, wu_ref, wd_ref, gf_ref, o_ref, act_ref = refs
    else:
        x_ref, g_ref, wg_ref, wu_ref, wd_ref, o_ref, act_ref = refs
    x = x_ref[...]
    h = _rms(x, g_ref[...]).astype(BF16)
    for j in range(0, D_FF, COL_CHUNK):
        _ffn_chunk(h, wg_ref, wu_ref, act_ref, j)
    xo = x + _dot(act_ref[...], wd_ref[...])
    if final:
        xo = _rms(xo, gf_ref[...])
    o_ref[...] = xo


def _ffn_specs(layer):
    specs = [_layer_spec(layer, (1, D_MODEL)), _const_spec((D_MODEL, D_FF)), _const_spec((D_MODEL, D_FF)),
             _const_spec((D_FF, D_MODEL))]
    if layer == DEPTH - 1:
        specs.append(_const_spec((1, D_MODEL)))
    return specs


def _ffn_args(w, wb, layer):
    args = [w["norm_ffn_g"], wb["gate"], wb["up"], wb["down"]]
    if layer == DEPTH - 1:
        args.append(w["norm_final_g"])
    return args


def _ffn(x, w, wb, layer, tile):
    rows = x.shape[0]
    final = layer == DEPTH - 1
    row_spec = pl.BlockSpec((tile, D_MODEL), lambda i: (i, 0))
    return pl.pallas_call(
        functools.partial(_ffn_kernel, final=final),
        out_shape=jax.ShapeDtypeStruct((rows, D_MODEL), F32),
        grid=(rows // tile,),
        in_specs=[row_spec] + _ffn_specs(layer),
        out_specs=row_spec,
        scratch_shapes=[pltpu.VMEM((tile, D_FF), BF16)],
        compiler_params=_params(),
        name="ffn_final" if final else "ffn",
    )(x, *_ffn_args(w, wb, layer))


def _ffn_sample_kernel(*refs, n_chunks, final):
    x_ref, g_ref, wg_ref, wu_ref, wd_ref = refs[:5]
    gf_ref = refs[5] if final else None
    o_ref, wgo_ref, wuo_ref, wdo_ref, h_scr, acc_scr = refs[-6:]
    j = pl.program_id(0)

    @pl.when(j == 0)
    def _():
        x = x_ref[...]
        h_scr[...] = _rms(x, g_ref[...]).astype(BF16)
        acc_scr[...] = x

    wg = wg_ref[...].astype(BF16)
    wu = wu_ref[...].astype(BF16)
    wd = wd_ref[...].astype(BF16)
    wgo_ref[...] = wg
    wuo_ref[...] = wu
    wdo_ref[...] = wd
    h = h_scr[...]
    act = (jax.nn.silu(_dot(h, wg)) * _dot(h, wu)).astype(BF16)
    acc_scr[...] += _dot(act, wd)

    @pl.when(j == n_chunks - 1)
    def _():
        y = acc_scr[...]
        if final:
            y = _rms(y, gf_ref[...])
        o_ref[...] = y


def _ffn_sample(x, w, layer):
    rows = x.shape[0]
    final = layer == DEPTH - 1
    n_chunks = D_FF // COL_CHUNK
    col_in = pl.BlockSpec((None, D_MODEL, COL_CHUNK), lambda j: (layer, 0, j))
    row_in = pl.BlockSpec((None, COL_CHUNK, D_MODEL), lambda j: (layer, j, 0))
    col_out = pl.BlockSpec((D_MODEL, COL_CHUNK), lambda j: (0, j))
    row_out = pl.BlockSpec((COL_CHUNK, D_MODEL), lambda j: (j, 0))
    in_specs = [_const_spec((rows, D_MODEL)), _layer_spec(layer, (1, D_MODEL)), col_in, col_in, row_in]
    args = [x, w["norm_ffn_g"], w["ffn_w_gate"], w["ffn_w_up"], w["ffn_w_down"]]
    if final:
        in_specs.append(_const_spec((1, D_MODEL)))
        args.append(w["norm_final_g"])
    y, gate, up, down = pl.pallas_call(
        functools.partial(_ffn_sample_kernel, n_chunks=n_chunks, final=final),
        out_shape=(jax.ShapeDtypeStruct((rows, D_MODEL), F32), jax.ShapeDtypeStruct((D_MODEL, D_FF), BF16),
                   jax.ShapeDtypeStruct((D_MODEL, D_FF), BF16), jax.ShapeDtypeStruct((D_FF, D_MODEL), BF16)),
        grid=(n_chunks,),
        in_specs=in_specs,
        out_specs=(_const_spec((rows, D_MODEL)), col_out, col_out, row_out),
        scratch_shapes=[pltpu.VMEM((rows, D_MODEL), BF16), pltpu.VMEM((rows, D_MODEL), F32)],
        compiler_params=_params(),
        name="ffn_sample_final" if final else "ffn_sample",
    )(*args)
    return y, dict(gate=gate, up=up, down=down)


def _sgu_kernel(x_ref, g_ref, win_ref, bin_ref, lng_ref, lnb_ref, ws_ref, bsb_ref, wout_ref, bout_ref,
                o_ref, u_scr, v_scr, vn_scr, gated_scr):
    tile = x_ref.shape[0]
    h = _rms(x_ref[...], g_ref[...]).astype(BF16)

    s1 = jnp.zeros((tile, LANES), F32)
    s2 = jnp.zeros((tile, LANES), F32)
    for j in range(0, D_SGU, COL_CHUNK):
        z = _dot(h, win_ref[:, D_SGU + j:D_SGU + j + COL_CHUNK]) + bin_ref[:, D_SGU + j:D_SGU + j + COL_CHUNK]
        v = _gelu(z)
        s1 = s1 + _lane_partial(v)
        s2 = s2 + _lane_partial(v * v)
        v_scr[:, j:j + COL_CHUNK] = v
    mean, rstd = _layer_norm_stats(s1, s2, D_SGU)
    for j in range(0, D_SGU, COL_CHUNK):
        u_scr[:, j:j + COL_CHUNK] = _dot(h, win_ref[:, j:j + COL_CHUNK]) + bin_ref[:, j:j + COL_CHUNK]
        vn = (v_scr[:, j:j + COL_CHUNK] - mean) * rstd * lng_ref[:, j:j + COL_CHUNK] + lnb_ref[:, j:j + COL_CHUNK]
        vn_scr[:, j:j + COL_CHUNK] = vn.astype(BF16)

    o_ref[...] = x_ref[...] + bout_ref[...]
    for hd in range(SGU_HEADS):
        w = _causal(ws_ref[hd], SGU_CHUNK)
        bias = jnp.concatenate([bsb_ref[hd]] * (SGU_HEAD_DIM // LANES), axis=1)
        c0 = hd * SGU_HEAD_DIM
        for r0 in range(0, tile, SGU_CHUNK):
            mixed = _dot(w, vn_scr[r0:r0 + SGU_CHUNK, c0:c0 + SGU_HEAD_DIM]) + bias
            gated_scr[r0:r0 + SGU_CHUNK, c0:c0 + SGU_HEAD_DIM] = (
                _gelu(u_scr[r0:r0 + SGU_CHUNK, c0:c0 + SGU_HEAD_DIM]) * mixed).astype(BF16)
        o_ref[...] += _dot(gated_scr[:, c0:c0 + SGU_HEAD_DIM], wout_ref[c0:c0 + SGU_HEAD_DIM, :])


def _sgu(x, w, wb, layer, tile):
    rows = x.shape[0]
    j = layer // 2
    row_spec = pl.BlockSpec((tile, D_MODEL), lambda i: (i, 0))
    in_specs = [row_spec, _layer_spec(layer, (1, D_MODEL)), _const_spec((D_MODEL, 2 * D_SGU)),
                _layer_spec(j, (1, 2 * D_SGU)), _layer_spec(j, (1, D_SGU)), _layer_spec(j, (1, D_SGU)),
                _layer_spec(j, (SGU_HEADS, SGU_CHUNK, SGU_CHUNK)), _layer_spec(j, (SGU_HEADS, SGU_CHUNK, LANES)),
                _const_spec((D_SGU, D_MODEL)), _layer_spec(j, (1, D_MODEL))]
    bsb = jnp.broadcast_to(w["sgu_b_s"][:, :, :, None], w["sgu_b_s"].shape + (LANES,))
    return pl.pallas_call(
        _sgu_kernel,
        out_shape=jax.ShapeDtypeStruct((rows, D_MODEL), F32),
        grid=(rows // tile,),
        in_specs=in_specs,
        out_specs=row_spec,
        scratch_shapes=[pltpu.VMEM((tile, D_SGU), F32), pltpu.VMEM((tile, D_SGU), F32),
                        pltpu.VMEM((tile, D_SGU), BF16), pltpu.VMEM((tile, D_SGU), BF16)],
        compiler_params=_params(),
        name="sgu",
    )(x, w["norm_mix_g"], wb["w_in"], w["sgu_b_in"], w["sgu_ln_g"], w["sgu_ln_b"], w["sgu_w_s"], bsb,
      wb["w_out"], w["sgu_b_out"])


def _sgu_sample_kernel(x_ref, g_ref, win_ref, bin_ref, lng_ref, lnb_ref, ws_ref, bsb_ref, wout_ref, bout_ref,
                       o_ref, v_ref, wino_ref, wouto_ref,
                       h_scr, u_scr, v_scr, s1_scr, s2_scr, gated_scr, acc_scr, *, chunk_len):
    s = pl.program_id(0)
    rows = x_ref.shape[0]
    nh = SGU_HEADS

    @pl.when(s == 0)
    def _():
        h_scr[...] = _rms(x_ref[...], g_ref[...]).astype(BF16)
        s1_scr[...] = jnp.zeros(s1_scr.shape, F32)
        s2_scr[...] = jnp.zeros(s2_scr.shape, F32)

    @pl.when(s < 2 * nh)
    def _():
        wb = win_ref[...].astype(BF16)
        wino_ref[...] = wb
        z = _dot(h_scr[...], wb) + bin_ref[...]

        @pl.when(s < nh)
        def _():
            v = _gelu(z)
            s1_scr[...] += _lane_partial(v)
            s2_scr[...] += _lane_partial(v * v)
            v_scr[s] = v

        @pl.when(s >= nh)
        def _():
            u_scr[s - nh] = z

    @pl.when(s == 2 * nh)
    def _():
        mean, rstd = _layer_norm_stats(s1_scr[...], s2_scr[...], D_SGU)
        for hd in range(nh):
            c0 = hd * SGU_HEAD_DIM
            vn = (v_scr[hd] - mean) * rstd * lng_ref[:, c0:c0 + SGU_HEAD_DIM] + lnb_ref[:, c0:c0 + SGU_HEAD_DIM]
            v_ref[:, c0:c0 + SGU_HEAD_DIM] = vn
            vnb = vn.astype(BF16)
            w = _causal(ws_ref[hd], chunk_len)
            bias = jnp.concatenate([bsb_ref[hd]] * (SGU_HEAD_DIM // LANES), axis=1)
            for r0 in range(0, rows, chunk_len):
                mixed = _dot(w, vnb[r0:r0 + chunk_len]) + bias
                gated_scr[hd, r0:r0 + chunk_len, :] = (_gelu(u_scr[hd, r0:r0 + chunk_len, :]) * mixed).astype(BF16)
        acc_scr[...] = x_ref[...] + bout_ref[...]

    @pl.when(s >= 2 * nh)
    def _():
        wb = wout_ref[...].astype(BF16)
        wouto_ref[...] = wb
        acc_scr[...] += _dot(gated_scr[s - 2 * nh], wb)

    @pl.when(s == 3 * nh - 1)
    def _():
        o_ref[...] = acc_scr[...]


def _sgu_sample(x, w, layer, chunk_len):
    rows = x.shape[0]
    j = layer // 2
    nh = SGU_HEADS
    hw = SGU_HEAD_DIM
    in_col = lambda s: (jnp.minimum(s, 2 * nh - 1) + nh) % (2 * nh)
    out_row = lambda s: jnp.maximum(s - 2 * nh, 0)
    ws = w["sgu_w_s"][:, :, :chunk_len, :chunk_len]
    bsb = jnp.broadcast_to(w["sgu_b_s"][:, :, :chunk_len, None], ws.shape[:3] + (LANES,))
    in_specs = [_const_spec((rows, D_MODEL)), _layer_spec(layer, (1, D_MODEL)),
                pl.BlockSpec((None, D_MODEL, hw), lambda s: (j, 0, in_col(s))),
                pl.BlockSpec((None, 1, hw), lambda s: (j, 0, in_col(s))),
                _layer_spec(j, (1, D_SGU)), _layer_spec(j, (1, D_SGU)),
                _layer_spec(j, (nh, chunk_len, chunk_len)), _layer_spec(j, (nh, chunk_len, LANES)),
                pl.BlockSpec((None, hw, D_MODEL), lambda s: (j, out_row(s), 0)),
                _layer_spec(j, (1, D_MODEL))]
    y, v, w_in, w_out = pl.pallas_call(
        functools.partial(_sgu_sample_kernel, chunk_len=chunk_len),
        out_shape=(jax.ShapeDtypeStruct((rows, D_MODEL), F32), jax.ShapeDtypeStruct((rows, D_SGU), F32),
                   jax.ShapeDtypeStruct((D_MODEL, 2 * D_SGU), BF16), jax.ShapeDtypeStruct((D_SGU, D_MODEL), BF16)),
        grid=(3 * nh,),
        in_specs=in_specs,
        out_specs=(_const_spec((rows, D_MODEL)), _const_spec((rows, D_SGU)),
                   pl.BlockSpec((D_MODEL, hw), lambda s: (0, in_col(s))),
                   pl.BlockSpec((hw, D_MODEL), lambda s: (out_row(s), 0))),
        scratch_shapes=[pltpu.VMEM((rows, D_MODEL), BF16), pltpu.VMEM((nh, rows, hw), F32),
                        pltpu.VMEM((nh, rows, hw), F32), pltpu.VMEM((rows, LANES), F32),
                        pltpu.VMEM((rows, LANES), F32), pltpu.VMEM((nh, rows, hw), BF16),
                        pltpu.VMEM((rows, D_MODEL), F32)],
        compiler_params=_params(),
        name="sgu_sample",
    )(x, w["norm_mix_g"], w["sgu_w_in"], w["sgu_b_in"], w["sgu_ln_g"], w["sgu_ln_b"], ws, bsb,
      w["sgu_w_out"], w["sgu_b_out"])
    return y, v, dict(w_in=w_in, w_out=w_out)


def _depthwise(pad_scr, conv_scr, wdw_ref, bdw_ref, s, l, r0, out_row, row_block):
    acc = jnp.broadcast_to(bdw_ref[l], (row_block, LANES))
    for k in range(CONV_WIDTH):
        acc = acc + wdw_ref[l, k:k + 1, :] * pad_scr[s, l, r0 + k + HALO_PAD:r0 + k + HALO_PAD + row_block, :]
    conv_scr[l, out_row:out_row + row_block, :] = acc


def _conv_norm_act(conv_scr, lng_ref, lnb_ref):
    c = jnp.concatenate([conv_scr[l] for l in range(N_LANE_TILES)], axis=1)
    mu = jnp.mean(c, axis=-1, keepdims=True)
    cc = c - mu
    var = jnp.mean(cc * cc, axis=-1, keepdims=True)
    cn = cc * lax.rsqrt(var + LN_EPS) * lng_ref[...] + lnb_ref[...]
    return jax.nn.silu(cn).astype(BF16)


def _conv_state(pad_scr, s, seg_len):
    return jnp.concatenate(
        [pad_scr[s, l, HALO + seg_len - CONV_STATE:HALO + seg_len, :] for l in range(N_LANE_TILES)], axis=1)


def _conv_mixer(x, g_ref, w1a_ref, w1b_ref, b1_ref, wdw_ref, bdw_ref, lng_ref, lnb_ref, w2_ref, b2_ref, st_ref,
                pad_scr, conv_scr, *, between):
    seg_len = x.shape[0]
    h = _rms(x, g_ref[...]).astype(BF16)
    for j in range(0, D_MODEL, COL_CHUNK):
        a1 = _dot(h, w1a_ref[:, j:j + COL_CHUNK]) + b1_ref[:, j:j + COL_CHUNK]
        a2 = _dot(h, w1b_ref[:, j:j + COL_CHUNK]) + b1_ref[:, D_MODEL + j:D_MODEL + j + COL_CHUNK]
        glu = a1 * jax.nn.sigmoid(a2)
        for jj in range(0, COL_CHUNK, LANES):
            pad_scr[0, (j + jj) // LANES, HALO:HALO + seg_len, :] = glu[:, jj:jj + LANES]
        for l in range(j // LANES, (j + COL_CHUNK) // LANES):
            for r0 in range(0, seg_len, CONV_ROW_BLOCK):
                _depthwise(pad_scr, conv_scr, wdw_ref, bdw_ref, 0, l, r0, r0, CONV_ROW_BLOCK)
                between(l, r0)

    st_ref[0] = _conv_state(pad_scr, 0, seg_len)
    pad_scr[:, :, 0:HALO, :] = pad_scr[:, :, seg_len:seg_len + HALO, :]
    act = _conv_norm_act(conv_scr, lng_ref, lnb_ref)
    return x + _dot(act, w2_ref[...]) + b2_ref[...]


def _conv_sample_kernel(x_ref, g_ref, w1a_ref, w1b_ref, b1a_ref, b1b_ref, past_ref, wdw_ref, bdw_ref, lng_ref, lnb_ref,
                        w2_ref, b2_ref, o_ref, st_ref, w1ao_ref, w1bo_ref, w2o_ref,
                        h_scr, pad_scr, conv_scr, act_scr, acc_scr, *, nseg, seg_len):
    s = pl.program_id(0)
    n = D_MODEL // COL_CHUNK

    @pl.when(s == 0)
    def _():
        h_scr[...] = _rms(x_ref[...], g_ref[...]).astype(BF16)
        for l in range(N_LANE_TILES):
            pad_scr[:, l, HALO_PAD:HALO, :] = past_ref[:, :, l * LANES:(l + 1) * LANES]

    @pl.when(s < n)
    def _():
        wa = w1a_ref[...].astype(BF16)
        wb = w1b_ref[...].astype(BF16)
        w1ao_ref[...] = wa
        w1bo_ref[...] = wb
        h = h_scr[...]
        glu = (_dot(h, wa) + b1a_ref[...]) * jax.nn.sigmoid(_dot(h, wb) + b1b_ref[...])
        for jj in range(COL_CHUNK // LANES):
            l = s * (COL_CHUNK // LANES) + jj
            for q in range(nseg):
                pad_scr[q, l, HALO:HALO + seg_len, :] = glu[q * seg_len:(q + 1) * seg_len, jj * LANES:(jj + 1) * LANES]
            for q in range(nseg):
                _depthwise(pad_scr, conv_scr, wdw_ref, bdw_ref, q, l, 0, q * seg_len, seg_len)

    @pl.when(s == n)
    def _():
        for q in range(nseg):
            st_ref[q] = _conv_state(pad_scr, q, seg_len)
        act = _conv_norm_act(conv_scr, lng_ref, lnb_ref)
        for r in range(n):
            act_scr[r] = act[:, r * COL_CHUNK:(r + 1) * COL_CHUNK]
        acc_scr[...] = x_ref[...] + b2_ref[...]

    @pl.when(s >= n)
    def _():
        wb = w2_ref[...].astype(BF16)
        w2o_ref[...] = wb
        acc_scr[...] += _dot(act_scr[s - n], wb)

    @pl.when(s == 2 * n - 1)
    def _():
        o_ref[...] = acc_scr[...]


def _conv_sample(x, state, w, layer):
    rows = x.shape[0]
    nseg = state.shape[1]
    seg_len = rows // nseg
    j = layer // 2
    st_shape = state.shape[1:]
    n = D_MODEL // COL_CHUNK
    col = lambda s: jnp.minimum(s, n - 1)
    krow = lambda s: jnp.maximum(s - n, 0)
    in_specs = [_const_spec((rows, D_MODEL)), _layer_spec(layer, (1, D_MODEL)),
                pl.BlockSpec((None, D_MODEL, COL_CHUNK), lambda s: (j, 0, col(s))),
                pl.BlockSpec((None, D_MODEL, COL_CHUNK), lambda s: (j, 0, n + col(s))),
                pl.BlockSpec((None, 1, COL_CHUNK), lambda s: (j, 0, col(s))),
                pl.BlockSpec((None, 1, COL_CHUNK), lambda s: (j, 0, n + col(s))),
                _layer_spec(j, st_shape),
                _layer_spec(j, (N_LANE_TILES, CONV_WIDTH, LANES)), _layer_spec(j, (N_LANE_TILES, 1, LANES)),
                _layer_spec(j, (1, D_MODEL)), _layer_spec(j, (1, D_MODEL)),
                pl.BlockSpec((None, COL_CHUNK, D_MODEL), lambda s: (j, krow(s), 0)),
                _layer_spec(j, (1, D_MODEL))]
    y, st, w1a, w1b, w2 = pl.pallas_call(
        functools.partial(_conv_sample_kernel, nseg=nseg, seg_len=seg_len),
        out_shape=(jax.ShapeDtypeStruct((rows, D_MODEL), F32), jax.ShapeDtypeStruct(st_shape, F32),
                   jax.ShapeDtypeStruct((D_MODEL, D_MODEL), BF16), jax.ShapeDtypeStruct((D_MODEL, D_MODEL), BF16),
                   jax.ShapeDtypeStruct((D_MODEL, D_MODEL), BF16)),
        grid=(2 * n,),
        in_specs=in_specs,
        out_specs=(_const_spec((rows, D_MODEL)), _const_spec(st_shape),
                   pl.BlockSpec((D_MODEL, COL_CHUNK), lambda s: (0, col(s))),
                   pl.BlockSpec((D_MODEL, COL_CHUNK), lambda s: (0, col(s))),
                   pl.BlockSpec((COL_CHUNK, D_MODEL), lambda s: (krow(s), 0))),
        scratch_shapes=[pltpu.VMEM((rows, D_MODEL), BF16),
                        pltpu.VMEM((nseg, N_LANE_TILES, HALO + seg_len, LANES), F32),
                        pltpu.VMEM((N_LANE_TILES, rows, LANES), F32),
                        pltpu.VMEM((n, rows, COL_CHUNK), BF16), pltpu.VMEM((rows, D_MODEL), F32)],
        compiler_params=_params(),
        name="conv_sample",
    )(x, w["norm_mix_g"], w["conv_w_pw1"], w["conv_w_pw1"], w["conv_b_pw1"], w["conv_b_pw1"], state,
      w["conv_w_dw"], w["conv_b_dw"], w["conv_ln_g"], w["conv_ln_b"], w["conv_w_pw2"], w["conv_b_pw2"])
    return y, st, dict(w1a=w1a, w1b=w1b, w2=w2)


_FFN_SLOTS = ((0, 128), (0, 384), (1, 128), (2, 128), (2, 384), (3, 128), (4, 128), (4, 384), (5, 128),
              (6, 128), (7, 128))
assert len(_FFN_SLOTS) == D_FF // COL_CHUNK


def _conv_ffn_kernel(*refs, n_tiles, tiles_per_seq, final):
    x_ref = refs[0]
    conv_params, (gf_ref, wg_ref, wu_ref, wd_ref) = refs[1:11], refs[11:15]
    rest = refs[15:]
    if final:
        gfin_ref, rest = rest[0], rest[1:]
    o_ref, st_ref, pad_scr, conv_scr, xmid_scr, xprev_scr, act_scr = rest
    i = pl.program_id(0)

    @pl.when(i == 0)
    def _():
        xmid_scr[...] = jnp.zeros(xmid_scr.shape, F32)

    xprev_scr[...] = xmid_scr[...]

    @pl.when(jnp.minimum(i, n_tiles - 1) % tiles_per_seq == 0)
    def _():
        pad_scr[:, :, 0:HALO, :] = jnp.zeros((1, N_LANE_TILES, HALO, LANES), F32)

    xp = xprev_scr[...]
    hf = _rms(xp, gf_ref[...]).astype(BF16)

    def between(l, r0):
        if (l, r0) in _FFN_SLOTS:
            _ffn_chunk(hf, wg_ref, wu_ref, act_scr, _FFN_SLOTS.index((l, r0)) * COL_CHUNK)

    xmid_scr[...] = _conv_mixer(x_ref[...], *conv_params, st_ref, pad_scr, conv_scr, between=between)
    y = xp + _dot(act_scr[...], wd_ref[...])
    if final:
        y = _rms(y, gfin_ref[...])
    o_ref[...] = y


def _conv_ffn(x, w, wb_conv, wb_ffn, layer, tile, seq_len):
    rows = x.shape[0]
    n_tiles = rows // tile
    tiles_per_seq = seq_len // tile
    final = layer == DEPTH - 1
    j = layer // 2
    sq = (D_MODEL, D_MODEL)
    conv_specs = [_layer_spec(layer, (1, D_MODEL)), _const_spec(sq), _const_spec(sq), _layer_spec(j, (1, 2 * D_MODEL)),
                  _layer_spec(j, (N_LANE_TILES, CONV_WIDTH, LANES)), _layer_spec(j, (N_LANE_TILES, 1, LANES)),
                  _layer_spec(j, (1, D_MODEL)), _layer_spec(j, (1, D_MODEL)), _const_spec(sq),
                  _layer_spec(j, (1, D_MODEL))]
    conv_args = [w["norm_mix_g"], wb_conv["w1a"], wb_conv["w1b"], w["conv_b_pw1"], w["conv_w_dw"], w["conv_b_dw"],
                 w["conv_ln_g"], w["conv_ln_b"], wb_conv["w2"], w["conv_b_pw2"]]
    return pl.pallas_call(
        functools.partial(_conv_ffn_kernel, n_tiles=n_tiles, tiles_per_seq=tiles_per_seq, final=final),
        out_shape=(jax.ShapeDtypeStruct((rows, D_MODEL), F32),
                   jax.ShapeDtypeStruct((rows // seq_len, CONV_STATE, D_MODEL), F32)),
        grid=(n_tiles + 1,),
        in_specs=[pl.BlockSpec((tile, D_MODEL), lambda i: (jnp.minimum(i, n_tiles - 1), 0))]
        + conv_specs + _ffn_specs(layer),
        out_specs=(pl.BlockSpec((tile, D_MODEL), lambda i: (jnp.maximum(i - 1, 0), 0)),
                   pl.BlockSpec((1, CONV_STATE, D_MODEL),
                                lambda i: (jnp.minimum(i, n_tiles - 1) // tiles_per_seq, 0, 0))),
        scratch_shapes=[pltpu.VMEM((1, N_LANE_TILES, HALO + tile, LANES), F32),
                        pltpu.VMEM((N_LANE_TILES, tile, LANES), F32),
                        pltpu.VMEM((tile, D_MODEL), F32), pltpu.VMEM((tile, D_MODEL), F32),
                        pltpu.VMEM((tile, D_FF), BF16)],
        compiler_params=_params(),
        name="conv_ffn_final" if final else "conv_ffn",
    )(x, *conv_args, *_ffn_args(w, wb_ffn, layer))


def _lane_tile_major(a):
    return a.reshape(a.shape[0], a.shape[1], N_LANE_TILES, LANES).transpose(0, 2, 1, 3)


def _trunk(x_prompt, x_sample, state_conv, w):
    nb_p, t_p, _ = x_prompt.shape
    nb_s, t_s, _ = x_sample.shape
    xp = x_prompt.reshape(nb_p * t_p, D_MODEL)
    xs = x_sample.reshape(nb_s * t_s, D_MODEL)
    conv_p, conv_s, sgu_vs = [], [], []
    for layer in range(DEPTH):
        if layer % 2 == 0:
            xs, v, wb_mix = _sgu_sample(xs, w, layer, t_s)
            sgu_vs.append(v.reshape(nb_s, t_s, D_SGU))
            xs, wb_ffn = _ffn_sample(xs, w, layer)
            xp = _sgu(xp, w, wb_mix, layer, PROMPT_TILE)
            xp = _ffn(xp, w, wb_ffn, layer, PROMPT_TILE_FFN)
        else:
            xs, st_s, wb_mix = _conv_sample(xs, state_conv, w, layer)
            xs, wb_ffn = _ffn_sample(xs, w, layer)
            xp, st_p = _conv_ffn(xp, w, wb_mix, wb_ffn, layer, PROMPT_TILE, t_p)
            conv_p.append(st_p)
            conv_s.append(st_s)
    return (xp.reshape(nb_p, t_p, D_MODEL), xs.reshape(nb_s, t_s, D_MODEL), jnp.stack(conv_p), jnp.stack(conv_s),
            jnp.stack(sgu_vs))


def kernel(x_prompt, x_sample, state_conv, norm_mix_g, norm_ffn_g, norm_final_g, sgu_w_in, sgu_b_in, sgu_ln_g, sgu_ln_b, sgu_w_s, sgu_b_s, sgu_w_out, sgu_b_out, conv_w_pw1, conv_b_pw1, conv_w_dw, conv_b_dw, conv_ln_g, conv_ln_b, conv_w_pw2, conv_b_pw2, ffn_w_gate, ffn_w_up, ffn_w_down):
    assert CONV_STATE <= x_sample.shape[1] <= SGU_CHUNK and x_prompt.shape[1] % PROMPT_TILE == 0
    assert (x_prompt.shape[0] * x_prompt.shape[1]) % PROMPT_TILE_FFN == 0
    row = lambda a: a.reshape(a.shape[0], 1, a.shape[1])
    w = dict(norm_mix_g=row(norm_mix_g), norm_ffn_g=row(norm_ffn_g), norm_final_g=norm_final_g.reshape(1, D_MODEL),
             sgu_w_in=sgu_w_in, sgu_b_in=row(sgu_b_in), sgu_ln_g=row(sgu_ln_g), sgu_ln_b=row(sgu_ln_b),
             sgu_w_s=sgu_w_s, sgu_b_s=sgu_b_s, sgu_w_out=sgu_w_out, sgu_b_out=row(sgu_b_out),
             conv_w_pw1=conv_w_pw1, conv_b_pw1=row(conv_b_pw1), conv_w_dw=_lane_tile_major(conv_w_dw),
             conv_b_dw=_lane_tile_major(row(conv_b_dw)), conv_ln_g=row(conv_ln_g), conv_ln_b=row(conv_ln_b),
             conv_w_pw2=conv_w_pw2, conv_b_pw2=row(conv_b_pw2),
             ffn_w_gate=ffn_w_gate, ffn_w_up=ffn_w_up, ffn_w_down=ffn_w_down)
    return _trunk(x_prompt, x_sample, state_conv, w)
```

```python
import functools
import math

import jax
import jax.numpy as jnp
from jax import lax
from jax.experimental import pallas as pl
from jax.experimental.pallas import tpu as pltpu

D_MODEL = 1024
DEPTH = 4
SGU_CHUNK = 128
SGU_HEADS = 4
D_SGU = 3 * D_MODEL
SGU_HEAD_DIM = D_SGU // SGU_HEADS
CONV_WIDTH = 31
CONV_STATE = CONV_WIDTH - 1
D_FF = 2816
RMS_EPS = 1e-6
LN_EPS = 1e-5

LANES = 128
N_LANE_TILES = D_MODEL // LANES
HALO = 32
HALO_PAD = HALO - CONV_STATE
COL_CHUNK = 256
N_COL_CHUNKS = D_MODEL // COL_CHUNK
_SQUARE_CHUNKED = (N_COL_CHUNKS, D_MODEL, COL_CHUNK)
CONV_ROW_BLOCK = 64
VMEM_LIMIT_BYTES = 56 * 1024 * 1024

PROMPT_TILE = 512
PROMPT_TILE_FFN = 1024

F32 = jnp.float32
BF16 = jnp.bfloat16

_GELU_C = math.sqrt(2.0 / math.pi)


def _rms(x, g):
    return x * lax.rsqrt(jnp.mean(x * x, axis=-1, keepdims=True) + RMS_EPS) * g


def _gelu(z):
    w = (z * z) * (_GELU_C * 0.044715) + _GELU_C
    hz = 0.5 * z
    return hz + hz * jnp.tanh(z * w)


def _dot(a, b):
    return jnp.dot(a, b, preferred_element_type=F32)


def _dot_cols(a, w_ref, k0=0, k1=None):
    return jnp.concatenate([_dot(a, w_ref[c, k0:k1, :]) for c in range(w_ref.shape[0])], axis=1)


def _store_cols(o_ref, wb):
    for c in range(o_ref.shape[0]):
        o_ref[c] = wb[:, c * COL_CHUNK:(c + 1) * COL_CHUNK]


def _lane_partial(v):
    acc = v[:, :LANES]
    for k in range(LANES, v.shape[1], LANES):
        acc = acc + v[:, k:k + LANES]
    return acc


def _layer_norm_stats(s1, s2, n):
    mean = jnp.sum(s1, axis=-1, keepdims=True) * (1.0 / n)
    var = jnp.sum(s2, axis=-1, keepdims=True) * (1.0 / n) - mean * mean
    return mean, lax.rsqrt(var + LN_EPS)


def _causal(ws, chunk_len):
    row = lax.broadcasted_iota(jnp.int32, (chunk_len, chunk_len), 0)
    col = lax.broadcasted_iota(jnp.int32, (chunk_len, chunk_len), 1)
    return jnp.where(row >= col, ws, 0.0).astype(BF16)


def _layer_spec(layer, shape):
    nd = len(shape)
    return pl.BlockSpec((None,) + tuple(shape), lambda i: (layer,) + (0,) * nd, pipeline_mode=pl.Buffered(1))


def _const_spec(shape):
    nd = len(shape)
    return pl.BlockSpec(shape, lambda i: (0,) * nd, pipeline_mode=pl.Buffered(1))


def _params():
    return pltpu.CompilerParams(dimension_semantics=("arbitrary",), vmem_limit_bytes=VMEM_LIMIT_BYTES)


def _ffn_chunk(h, wg_ref, wu_ref, act_ref, j):
    g = _dot(h, wg_ref[:, j:j + COL_CHUNK])
    u = _dot(h, wu_ref[:, j:j + COL_CHUNK])
    act_ref[:, j:j + COL_CHUNK] = (jax.nn.silu(g) * u).astype(BF16)


def _ffn_kernel(*refs, final):
    if final:
        x_ref, g_ref, wg_ref, wu_ref, wd_ref, gf_ref, o_ref, act_ref = refs
    else:
        x_ref, g_ref, wg_ref, wu_ref, wd_ref, o_ref, act_ref = refs
    x = x_ref[...]
    h = _rms(x, g_ref[...]).astype(BF16)
    for j in range(0, D_FF, COL_CHUNK):
        _ffn_chunk(h, wg_ref, wu_ref, act_ref, j)
    xo = x + _dot_cols(act_ref[...], wd_ref)
    if final:
        xo = _rms(xo, gf_ref[...])
    o_ref[...] = xo


def _ffn_specs(layer):
    specs = [_layer_spec(layer, (1, D_MODEL)), _const_spec((D_MODEL, D_FF)), _const_spec((D_MODEL, D_FF)),
             _const_spec((N_COL_CHUNKS, D_FF, COL_CHUNK))]
    if layer == DEPTH - 1:
        specs.append(_const_spec((1, D_MODEL)))
    return specs


def _ffn_args(w, wb, layer):
    args = [w["norm_ffn_g"], wb["gate"], wb["up"], wb["down"]]
    if layer == DEPTH - 1:
        args.append(w["norm_final_g"])
    return args


def _ffn(x, w, wb, layer, tile):
    rows = x.shape[0]
    final = layer == DEPTH - 1
    row_spec = pl.BlockSpec((tile, D_MODEL), lambda i: (i, 0))
    return pl.pallas_call(
        functools.partial(_ffn_kernel, final=final),
        out_shape=jax.ShapeDtypeStruct((rows, D_MODEL), F32),
        grid=(rows // tile,),
        in_specs=[row_spec] + _ffn_specs(layer),
        out_specs=row_spec,
        scratch_shapes=[pltpu.VMEM((tile, D_FF), BF16)],
        compiler_params=_params(),
        name="ffn_final" if final else "ffn",
    )(x, *_ffn_args(w, wb, layer))


def _ffn_sample_kernel(*refs, n_chunks, final):
    x_ref, g_ref, wg_ref, wu_ref, wd_ref = refs[:5]
    gf_ref = refs[5] if final else None
    o_ref, wgo_ref, wuo_ref, wdo_ref, h_scr, acc_scr = refs[-6:]
    j = pl.program_id(0)

    @pl.when(j == 0)
    def _():
        x = x_ref[...]
        h_scr[...] = _rms(x, g_ref[...]).astype(BF16)
        acc_scr[...] = x

    wg = wg_ref[...].astype(BF16)
    wu = wu_ref[...].astype(BF16)
    wd = wd_ref[...].astype(BF16)
    wgo_ref[...] = wg
    wuo_ref[...] = wu
    _store_cols(wdo_ref, wd)
    h = h_scr[...]
    act = (jax.nn.silu(_dot(h, wg)) * _dot(h, wu)).astype(BF16)
    acc_scr[...] += _dot(act, wd)

    @pl.when(j == n_chunks - 1)
    def _():
        y = acc_scr[...]
        if final:
            y = _rms(y, gf_ref[...])
        o_ref[...] = y


def _ffn_sample(x, w, layer):
    rows = x.shape[0]
    final = layer == DEPTH - 1
    n_chunks = D_FF // COL_CHUNK
    col_in = pl.BlockSpec((None, D_MODEL, COL_CHUNK), lambda j: (layer, 0, j))
    row_in = pl.BlockSpec((None, COL_CHUNK, D_MODEL), lambda j: (layer, j, 0))
    col_out = pl.BlockSpec((D_MODEL, COL_CHUNK), lambda j: (0, j))
    row_out = pl.BlockSpec((N_COL_CHUNKS, COL_CHUNK, COL_CHUNK), lambda j: (0, j, 0))
    in_specs = [_const_spec((rows, D_MODEL)), _layer_spec(layer, (1, D_MODEL)), col_in, col_in, row_in]
    args = [x, w["norm_ffn_g"], w["ffn_w_gate"], w["ffn_w_up"], w["ffn_w_down"]]
    if final:
        in_specs.append(_const_spec((1, D_MODEL)))
        args.append(w["norm_final_g"])
    y, gate, up, down = pl.pallas_call(
        functools.partial(_ffn_sample_kernel, n_chunks=n_chunks, final=final),
        out_shape=(jax.ShapeDtypeStruct((rows, D_MODEL), F32), jax.ShapeDtypeStruct((D_MODEL, D_FF), BF16),
                   jax.ShapeDtypeStruct((D_MODEL, D_FF), BF16), jax.ShapeDtypeStruct((N_COL_CHUNKS, D_FF, COL_CHUNK), BF16)),
        grid=(n_chunks,),
        in_specs=in_specs,
        out_specs=(_const_spec((rows, D_MODEL)), col_out, col_out, row_out),
        scratch_shapes=[pltpu.VMEM((rows, D_MODEL), BF16), pltpu.VMEM((rows, D_MODEL), F32)],
        compiler_params=_params(),
        name="ffn_sample_final" if final else "ffn_sample",
    )(*args)
    return y, dict(gate=gate, up=up, down=down)


def _sgu_kernel(x_ref, g_ref, win_ref, bin_ref, lng_ref, lnb_ref, ws_ref, bsb_ref, wout_ref, bout_ref,
                o_ref, u_scr, v_scr, vn_scr, gated_scr):
    tile = x_ref.shape[0]
    h = _rms(x_ref[...], g_ref[...]).astype(BF16)

    s1 = jnp.zeros((tile, LANES), F32)
    s2 = jnp.zeros((tile, LANES), F32)
    for j in range(0, D_SGU, COL_CHUNK):
        z = _dot(h, win_ref[(D_SGU + j) // COL_CHUNK]) + bin_ref[:, D_SGU + j:D_SGU + j + COL_CHUNK]
        v = _gelu(z)
        s1 = s1 + _lane_partial(v)
        s2 = s2 + _lane_partial(v * v)
        v_scr[:, j:j + COL_CHUNK] = v
    mean, rstd = _layer_norm_stats(s1, s2, D_SGU)
    for j in range(0, D_SGU, COL_CHUNK):
        u_scr[:, j:j + COL_CHUNK] = _dot(h, win_ref[j // COL_CHUNK]) + bin_ref[:, j:j + COL_CHUNK]
        vn = (v_scr[:, j:j + COL_CHUNK] - mean) * rstd * lng_ref[:, j:j + COL_CHUNK] + lnb_ref[:, j:j + COL_CHUNK]
        vn_scr[:, j:j + COL_CHUNK] = vn.astype(BF16)

    o_ref[...] = x_ref[...] + bout_ref[...]
    for hd in range(SGU_HEADS):
        w = _causal(ws_ref[hd], SGU_CHUNK)
        bias = jnp.concatenate([bsb_ref[hd]] * (SGU_HEAD_DIM // LANES), axis=1)
        c0 = hd * SGU_HEAD_DIM
        for r0 in range(0, tile, SGU_CHUNK):
            mixed = _dot(w, vn_scr[r0:r0 + SGU_CHUNK, c0:c0 + SGU_HEAD_DIM]) + bias
            gated_scr[r0:r0 + SGU_CHUNK, c0:c0 + SGU_HEAD_DIM] = (
                _gelu(u_scr[r0:r0 + SGU_CHUNK, c0:c0 + SGU_HEAD_DIM]) * mixed).astype(BF16)
        o_ref[...] += _dot_cols(gated_scr[:, c0:c0 + SGU_HEAD_DIM], wout_ref, c0, c0 + SGU_HEAD_DIM)


def _sgu(x, w, wb, layer, tile):
    rows = x.shape[0]
    j = layer // 2
    row_spec = pl.BlockSpec((tile, D_MODEL), lambda i: (i, 0))
    in_specs = [row_spec, _layer_spec(layer, (1, D_MODEL)), _const_spec((2 * D_SGU // COL_CHUNK, D_MODEL, COL_CHUNK)),
                _layer_spec(j, (1, 2 * D_SGU)), _layer_spec(j, (1, D_SGU)), _layer_spec(j, (1, D_SGU)),
                _layer_spec(j, (SGU_HEADS, SGU_CHUNK, SGU_CHUNK)), _layer_spec(j, (SGU_HEADS, SGU_CHUNK, LANES)),
                _const_spec((N_COL_CHUNKS, D_SGU, COL_CHUNK)), _layer_spec(j, (1, D_MODEL))]
    bsb = jnp.broadcast_to(w["sgu_b_s"][:, :, :, None], w["sgu_b_s"].shape + (LANES,))
    return pl.pallas_call(
        _sgu_kernel,
        out_shape=jax.ShapeDtypeStruct((rows, D_MODEL), F32),
        grid=(rows // tile,),
        in_specs=in_specs,
        out_specs=row_spec,
        scratch_shapes=[pltpu.VMEM((tile, D_SGU), F32), pltpu.VMEM((tile, D_SGU), F32),
                        pltpu.VMEM((tile, D_SGU), BF16), pltpu.VMEM((tile, D_SGU), BF16)],
        compiler_params=_params(),
        name="sgu",
    )(x, w["norm_mix_g"], wb["w_in"], w["sgu_b_in"], w["sgu_ln_g"], w["sgu_ln_b"], w["sgu_w_s"], bsb,
      wb["w_out"], w["sgu_b_out"])


def _sgu_sample_kernel(x_ref, g_ref, win_ref, bin_ref, lng_ref, lnb_ref, ws_ref, bsb_ref, wout_ref, bout_ref,
                       o_ref, v_ref, wino_ref, wouto_ref,
                       h_scr, u_scr, v_scr, s1_scr, s2_scr, gated_scr, acc_scr, *, chunk_len):
    s = pl.program_id(0)
    rows = x_ref.shape[0]
    nh = SGU_HEADS

    @pl.when(s == 0)
    def _():
        h_scr[...] = _rms(x_ref[...], g_ref[...]).astype(BF16)
        s1_scr[...] = jnp.zeros(s1_scr.shape, F32)
        s2_scr[...] = jnp.zeros(s2_scr.shape, F32)

    @pl.when(s < 2 * nh)
    def _():
        wb = win_ref[...].astype(BF16)
        _store_cols(wino_ref, wb)
        z = _dot(h_scr[...], wb) + bin_ref[...]

        @pl.when(s < nh)
        def _():
            v = _gelu(z)
            s1_scr[...] += _lane_partial(v)
            s2_scr[...] += _lane_partial(v * v)
            v_scr[s] = v

        @pl.when(s >= nh)
        def _():
            u_scr[s - nh] = z

    @pl.when(s == 2 * nh)
    def _():
        mean, rstd = _layer_norm_stats(s1_scr[...], s2_scr[...], D_SGU)
        for hd in range(nh):
            c0 = hd * SGU_HEAD_DIM
            vn = (v_scr[hd] - mean) * rstd * lng_ref[:, c0:c0 + SGU_HEAD_DIM] + lnb_ref[:, c0:c0 + SGU_HEAD_DIM]
            v_ref[:, c0:c0 + SGU_HEAD_DIM] = vn
            vnb = vn.astype(BF16)
            w = _causal(ws_ref[hd], chunk_len)
            bias = jnp.concatenate([bsb_ref[hd]] * (SGU_HEAD_DIM // LANES), axis=1)
            for r0 in range(0, rows, chunk_len):
                mixed = _dot(w, vnb[r0:r0 + chunk_len]) + bias
                gated_scr[hd, r0:r0 + chunk_len, :] = (_gelu(u_scr[hd, r0:r0 + chunk_len, :]) * mixed).astype(BF16)
        acc_scr[...] = x_ref[...] + bout_ref[...]

    @pl.when(s >= 2 * nh)
    def _():
        wb = wout_ref[...].astype(BF16)
        _store_cols(wouto_ref, wb)
        acc_scr[...] += _dot(gated_scr[s - 2 * nh], wb)

    @pl.when(s == 3 * nh - 1)
    def _():
        o_ref[...] = acc_scr[...]


def _sgu_sample(x, w, layer, chunk_len):
    rows = x.shape[0]
    j = layer // 2
    nh = SGU_HEADS
    hw = SGU_HEAD_DIM
    in_col = lambda s: (jnp.minimum(s, 2 * nh - 1) + nh) % (2 * nh)
    out_row = lambda s: jnp.maximum(s - 2 * nh, 0)
    ws = w["sgu_w_s"][:, :, :chunk_len, :chunk_len]
    bsb = jnp.broadcast_to(w["sgu_b_s"][:, :, :chunk_len, None], ws.shape[:3] + (LANES,))
    in_specs = [_const_spec((rows, D_MODEL)), _layer_spec(layer, (1, D_MODEL)),
                pl.BlockSpec((None, D_MODEL, hw), lambda s: (j, 0, in_col(s))),
                pl.BlockSpec((None, 1, hw), lambda s: (j, 0, in_col(s))),
                _layer_spec(j, (1, D_SGU)), _layer_spec(j, (1, D_SGU)),
                _layer_spec(j, (nh, chunk_len, chunk_len)), _layer_spec(j, (nh, chunk_len, LANES)),
                pl.BlockSpec((None, hw, D_MODEL), lambda s: (j, out_row(s), 0)),
                _layer_spec(j, (1, D_MODEL))]
    y, v, w_in, w_out = pl.pallas_call(
        functools.partial(_sgu_sample_kernel, chunk_len=chunk_len),
        out_shape=(jax.ShapeDtypeStruct((rows, D_MODEL), F32), jax.ShapeDtypeStruct((rows, D_SGU), F32),
                   jax.ShapeDtypeStruct((2 * D_SGU // COL_CHUNK, D_MODEL, COL_CHUNK), BF16),
                   jax.ShapeDtypeStruct((N_COL_CHUNKS, D_SGU, COL_CHUNK), BF16)),
        grid=(3 * nh,),
        in_specs=in_specs,
        out_specs=(_const_spec((rows, D_MODEL)), _const_spec((rows, D_SGU)),
                   pl.BlockSpec((hw // COL_CHUNK, D_MODEL, COL_CHUNK), lambda s: (in_col(s), 0, 0)),
                   pl.BlockSpec((N_COL_CHUNKS, hw, COL_CHUNK), lambda s: (0, out_row(s), 0))),
        scratch_shapes=[pltpu.VMEM((rows, D_MODEL), BF16), pltpu.VMEM((nh, rows, hw), F32),
                        pltpu.VMEM((nh, rows, hw), F32), pltpu.VMEM((rows, LANES), F32),
                        pltpu.VMEM((rows, LANES), F32), pltpu.VMEM((nh, rows, hw), BF16),
                        pltpu.VMEM((rows, D_MODEL), F32)],
        compiler_params=_params(),
        name="sgu_sample",
    )(x, w["norm_mix_g"], w["sgu_w_in"], w["sgu_b_in"], w["sgu_ln_g"], w["sgu_ln_b"], ws, bsb,
      w["sgu_w_out"], w["sgu_b_out"])
    return y, v, dict(w_in=w_in, w_out=w_out)


def _depthwise(pad_scr, conv_scr, wdw_ref, bdw_ref, s, l, r0, out_row, row_block):
    acc = jnp.broadcast_to(bdw_ref[l], (row_block, LANES))
    for k in range(CONV_WIDTH):
        acc = acc + wdw_ref[l, k:k + 1, :] * pad_scr[s, l, r0 + k + HALO_PAD:r0 + k + HALO_PAD + row_block, :]
    conv_scr[l, out_row:out_row + row_block, :] = acc


def _conv_norm_act(conv_scr, lng_ref, lnb_ref):
    c = jnp.concatenate([conv_scr[l] for l in range(N_LANE_TILES)], axis=1)
    mu = jnp.mean(c, axis=-1, keepdims=True)
    cc = c - mu
    var = jnp.mean(cc * cc, axis=-1, keepdims=True)
    cn = cc * lax.rsqrt(var + LN_EPS) * lng_ref[...] + lnb_ref[...]
    return jax.nn.silu(cn).astype(BF16)


def _conv_state(pad_scr, s, seg_len):
    return jnp.concatenate(
        [pad_scr[s, l, HALO + seg_len - CONV_STATE:HALO + seg_len, :] for l in range(N_LANE_TILES)], axis=1)


def _conv_mixer(x, g_ref, w1a_ref, w1b_ref, b1_ref, wdw_ref, bdw_ref, lng_ref, lnb_ref, w2_ref, b2_ref, st_ref,
                pad_scr, conv_scr, *, between):
    seg_len = x.shape[0]
    h = _rms(x, g_ref[...]).astype(BF16)
    for j in range(0, D_MODEL, COL_CHUNK):
        a1 = _dot(h, w1a_ref[j // COL_CHUNK]) + b1_ref[:, j:j + COL_CHUNK]
        a2 = _dot(h, w1b_ref[j // COL_CHUNK]) + b1_ref[:, D_MODEL + j:D_MODEL + j + COL_CHUNK]
        glu = a1 * jax.nn.sigmoid(a2)
        for jj in range(0, COL_CHUNK, LANES):
            pad_scr[0, (j + jj) // LANES, HALO:HALO + seg_len, :] = glu[:, jj:jj + LANES]
        for l in range(j // LANES, (j + COL_CHUNK) // LANES):
            for r0 in range(0, seg_len, CONV_ROW_BLOCK):
                _depthwise(pad_scr, conv_scr, wdw_ref, bdw_ref, 0, l, r0, r0, CONV_ROW_BLOCK)
                between(l, r0)

    st_ref[0] = _conv_state(pad_scr, 0, seg_len)
    pad_scr[:, :, 0:HALO, :] = pad_scr[:, :, seg_len:seg_len + HALO, :]
    act = _conv_norm_act(conv_scr, lng_ref, lnb_ref)
    return x + _dot_cols(act, w2_ref) + b2_ref[...]


def _conv_sample_kernel(x_ref, g_ref, w1a_ref, w1b_ref, b1a_ref, b1b_ref, past_ref, wdw_ref, bdw_ref, lng_ref, lnb_ref,
                        w2_ref, b2_ref, o_ref, st_ref, w1ao_ref, w1bo_ref, w2o_ref,
                        h_scr, pad_scr, conv_scr, act_scr, acc_scr, *, nseg, seg_len):
    s = pl.program_id(0)
    n = D_MODEL // COL_CHUNK

    @pl.when(s == 0)
    def _():
        h_scr[...] = _rms(x_ref[...], g_ref[...]).astype(BF16)
        for l in range(N_LANE_TILES):
            pad_scr[:, l, HALO_PAD:HALO, :] = past_ref[:, :, l * LANES:(l + 1) * LANES]

    @pl.when(s < n)
    def _():
        wa = w1a_ref[...].astype(BF16)
        wb = w1b_ref[...].astype(BF16)
        w1ao_ref[0] = wa
        w1bo_ref[0] = wb
        h = h_scr[...]
        glu = (_dot(h, wa) + b1a_ref[...]) * jax.nn.sigmoid(_dot(h, wb) + b1b_ref[...])
        for jj in range(COL_CHUNK // LANES):
            l = s * (COL_CHUNK // LANES) + jj
            for q in range(nseg):
                pad_scr[q, l, HALO:HALO + seg_len, :] = glu[q * seg_len:(q + 1) * seg_len, jj * LANES:(jj + 1) * LANES]
            for q in range(nseg):
                _depthwise(pad_scr, conv_scr, wdw_ref, bdw_ref, q, l, 0, q * seg_len, seg_len)

    @pl.when(s == n)
    def _():
        for q in range(nseg):
            st_ref[q] = _conv_state(pad_scr, q, seg_len)
        act = _conv_norm_act(conv_scr, lng_ref, lnb_ref)
        for r in range(n):
            act_scr[r] = act[:, r * COL_CHUNK:(r + 1) * COL_CHUNK]
        acc_scr[...] = x_ref[...] + b2_ref[...]

    @pl.when(s >= n)
    def _():
        wb = w2_ref[...].astype(BF16)
        _store_cols(w2o_ref, wb)
        acc_scr[...] += _dot(act_scr[s - n], wb)

    @pl.when(s == 2 * n - 1)
    def _():
        o_ref[...] = acc_scr[...]


def _conv_sample(x, state, w, layer):
    rows = x.shape[0]
    nseg = state.shape[1]
    seg_len = rows // nseg
    j = layer // 2
    st_shape = state.shape[1:]
    n = D_MODEL // COL_CHUNK
    col = lambda s: jnp.minimum(s, n - 1)
    krow = lambda s: jnp.maximum(s - n, 0)
    in_specs = [_const_spec((rows, D_MODEL)), _layer_spec(layer, (1, D_MODEL)),
                pl.BlockSpec((None, D_MODEL, COL_CHUNK), lambda s: (j, 0, col(s))),
                pl.BlockSpec((None, D_MODEL, COL_CHUNK), lambda s: (j, 0, n + col(s))),
                pl.BlockSpec((None, 1, COL_CHUNK), lambda s: (j, 0, col(s))),
                pl.BlockSpec((None, 1, COL_CHUNK), lambda s: (j, 0, n + col(s))),
                _layer_spec(j, st_shape),
                _layer_spec(j, (N_LANE_TILES, CONV_WIDTH, LANES)), _layer_spec(j, (N_LANE_TILES, 1, LANES)),
                _layer_spec(j, (1, D_MODEL)), _layer_spec(j, (1, D_MODEL)),
                pl.BlockSpec((None, COL_CHUNK, D_MODEL), lambda s: (j, krow(s), 0)),
                _layer_spec(j, (1, D_MODEL))]
    y, st, w1a, w1b, w2 = pl.pallas_call(
        functools.partial(_conv_sample_kernel, nseg=nseg, seg_len=seg_len),
        out_shape=(jax.ShapeDtypeStruct((rows, D_MODEL), F32), jax.ShapeDtypeStruct(st_shape, F32),
                   jax.ShapeDtypeStruct(_SQUARE_CHUNKED, BF16), jax.ShapeDtypeStruct(_SQUARE_CHUNKED, BF16),
                   jax.ShapeDtypeStruct(_SQUARE_CHUNKED, BF16)),
        grid=(2 * n,),
        in_specs=in_specs,
        out_specs=(_const_spec((rows, D_MODEL)), _const_spec(st_shape),
                   pl.BlockSpec((1, D_MODEL, COL_CHUNK), lambda s: (col(s), 0, 0)),
                   pl.BlockSpec((1, D_MODEL, COL_CHUNK), lambda s: (col(s), 0, 0)),
                   pl.BlockSpec((N_COL_CHUNKS, COL_CHUNK, COL_CHUNK), lambda s: (0, krow(s), 0))),
        scratch_shapes=[pltpu.VMEM((rows, D_MODEL), BF16),
                        pltpu.VMEM((nseg, N_LANE_TILES, HALO + seg_len, LANES), F32),
                        pltpu.VMEM((N_LANE_TILES, rows, LANES), F32),
                        pltpu.VMEM((n, rows, COL_CHUNK), BF16), pltpu.VMEM((rows, D_MODEL), F32)],
        compiler_params=_params(),
        name="conv_sample",
    )(x, w["norm_mix_g"], w["conv_w_pw1"], w["conv_w_pw1"], w["conv_b_pw1"], w["conv_b_pw1"], state,
      w["conv_w_dw"], w["conv_b_dw"], w["conv_ln_g"], w["conv_ln_b"], w["conv_w_pw2"], w["conv_b_pw2"])
    return y, st, dict(w1a=w1a, w1b=w1b, w2=w2)


_FFN_SLOTS = ((0, 128), (0, 384), (1, 128), (2, 128), (2, 384), (3, 128), (4, 128), (4, 384), (5, 128),
              (6, 128), (7, 128))
assert len(_FFN_SLOTS) == D_FF // COL_CHUNK


def _conv_ffn_kernel(*refs, n_tiles, tiles_per_seq, final):
    x_ref = refs[0]
    conv_params, (gf_ref, wg_ref, wu_ref, wd_ref) = refs[1:11], refs[11:15]
    rest = refs[15:]
    if final:
        gfin_ref, rest = rest[0], rest[1:]
    o_ref, st_ref, pad_scr, conv_scr, xmid_scr, xprev_scr, act_scr = rest
    i = pl.program_id(0)

    @pl.when(i == 0)
    def _():
        xmid_scr[...] = jnp.zeros(xmid_scr.shape, F32)

    xprev_scr[...] = xmid_scr[...]

    @pl.when(jnp.minimum(i, n_tiles - 1) % tiles_per_seq == 0)
    def _():
        pad_scr[:, :, 0:HALO, :] = jnp.zeros((1, N_LANE_TILES, HALO, LANES), F32)

    xp = xprev_scr[...]
    hf = _rms(xp, gf_ref[...]).astype(BF16)

    def between(l, r0):
        if (l, r0) in _FFN_SLOTS:
            _ffn_chunk(hf, wg_ref, wu_ref, act_scr, _FFN_SLOTS.index((l, r0)) * COL_CHUNK)

    xmid_scr[...] = _conv_mixer(x_ref[...], *conv_params, st_ref, pad_scr, conv_scr, between=between)
    y = xp + _dot_cols(act_scr[...], wd_ref)
    if final:
        y = _rms(y, gfin_ref[...])
    o_ref[...] = y


def _conv_ffn(x, w, wb_conv, wb_ffn, layer, tile, seq_len):
    rows = x.shape[0]
    n_tiles = rows // tile
    tiles_per_seq = seq_len // tile
    final = layer == DEPTH - 1
    j = layer // 2
    sq = _SQUARE_CHUNKED
    conv_specs = [_layer_spec(layer, (1, D_MODEL)), _const_spec(sq), _const_spec(sq), _layer_spec(j, (1, 2 * D_MODEL)),
                  _layer_spec(j, (N_LANE_TILES, CONV_WIDTH, LANES)), _layer_spec(j, (N_LANE_TILES, 1, LANES)),
                  _layer_spec(j, (1, D_MODEL)), _layer_spec(j, (1, D_MODEL)), _const_spec(sq),
                  _layer_spec(j, (1, D_MODEL))]
    conv_args = [w["norm_mix_g"], wb_conv["w1a"], wb_conv["w1b"], w["conv_b_pw1"], w["conv_w_dw"], w["conv_b_dw"],
                 w["conv_ln_g"], w["conv_ln_b"], wb_conv["w2"], w["conv_b_pw2"]]
    return pl.pallas_call(
        functools.partial(_conv_ffn_kernel, n_tiles=n_tiles, tiles_per_seq=tiles_per_seq, final=final),
        out_shape=(jax.ShapeDtypeStruct((rows, D_MODEL), F32),
                   jax.ShapeDtypeStruct((rows // seq_len, CONV_STATE, D_MODEL), F32)),
        grid=(n_tiles + 1,),
        in_specs=[pl.BlockSpec((tile, D_MODEL), lambda i: (jnp.minimum(i, n_tiles - 1), 0))]
        + conv_specs + _ffn_specs(layer),
        out_specs=(pl.BlockSpec((tile, D_MODEL), lambda i: (jnp.maximum(i - 1, 0), 0)),
                   pl.BlockSpec((1, CONV_STATE, D_MODEL),
                                lambda i: (jnp.minimum(i, n_tiles - 1) // tiles_per_seq, 0, 0))),
        scratch_shapes=[pltpu.VMEM((1, N_LANE_TILES, HALO + tile, LANES), F32),
                        pltpu.VMEM((N_LANE_TILES, tile, LANES), F32),
                        pltpu.VMEM((tile, D_MODEL), F32), pltpu.VMEM((tile, D_MODEL), F32),
                        pltpu.VMEM((tile, D_FF), BF16)],
        compiler_params=_params(),
        name="conv_ffn_final" if final else "conv_ffn",
    )(x, *conv_args, *_ffn_args(w, wb_ffn, layer))


def _lane_tile_major(a):
    return a.reshape(a.shape[0], a.shape[1], N_LANE_TILES, LANES).transpose(0, 2, 1, 3)


def _trunk(x_prompt, x_sample, state_conv, w):
    nb_p, t_p, _ = x_prompt.shape
    nb_s, t_s, _ = x_sample.shape
    xp = x_prompt.reshape(nb_p * t_p, D_MODEL)
    xs = x_sample.reshape(nb_s * t_s, D_MODEL)
    conv_p, conv_s, sgu_vs = [], [], []
    for layer in range(DEPTH):
        if layer % 2 == 0:
            xs, v, wb_mix = _sgu_sample(xs, w, layer, t_s)
            sgu_vs.append(v.reshape(nb_s, t_s, D_SGU))
            xs, wb_ffn = _ffn_sample(xs, w, layer)
            xp = _sgu(xp, w, wb_mix, layer, PROMPT_TILE)
            xp = _ffn(xp, w, wb_ffn, layer, PROMPT_TILE_FFN)
        else:
            xs, st_s, wb_mix = _conv_sample(xs, state_conv, w, layer)
            xs, wb_ffn = _ffn_sample(xs, w, layer)
            xp, st_p = _conv_ffn(xp, w, wb_mix, wb_ffn, layer, PROMPT_TILE, t_p)
            conv_p.append(st_p)
            conv_s.append(st_s)
    return (xp.reshape(nb_p, t_p, D_MODEL), xs.reshape(nb_s, t_s, D_MODEL), jnp.stack(conv_p), jnp.stack(conv_s),
            jnp.stack(sgu_vs))


def kernel(x_prompt, x_sample, state_conv, norm_mix_g, norm_ffn_g, norm_final_g, sgu_w_in, sgu_b_in, sgu_ln_g, sgu_ln_b, sgu_w_s, sgu_b_s, sgu_w_out, sgu_b_out, conv_w_pw1, conv_b_pw1, conv_w_dw, conv_b_dw, conv_ln_g, conv_ln_b, conv_w_pw2, conv_b_pw2, ffn_w_gate, ffn_w_up, ffn_w_down):
    assert CONV_STATE <= x_sample.shape[1] <= SGU_CHUNK and x_prompt.shape[1] % PROMPT_TILE == 0
    assert (x_prompt.shape[0] * x_prompt.shape[1]) % PROMPT_TILE_FFN == 0
    row = lambda a: a.reshape(a.shape[0], 1, a.shape[1])
    w = dict(norm_mix_g=row(norm_mix_g), norm_ffn_g=row(norm_ffn_g), norm_final_g=norm_final_g.reshape(1, D_MODEL),
             sgu_w_in=sgu_w_in, sgu_b_in=row(sgu_b_in), sgu_ln_g=row(sgu_ln_g), sgu_ln_b=row(sgu_ln_b),
             sgu_w_s=sgu_w_s, sgu_b_s=sgu_b_s, sgu_w_out=sgu_w_out, sgu_b_out=row(sgu_b_out),
             conv_w_pw1=conv_w_pw1, conv_b_pw1=row(conv_b_pw1), conv_w_dw=_lane_tile_major(conv_w_dw),
             conv_b_dw=_lane_tile_major(row(conv_b_dw)), conv_ln_g=row(conv_ln_g), conv_ln_b=row(conv_ln_b),
             conv_w_pw2=conv_w_pw2, conv_b_pw2=row(conv_b_pw2),
             ffn_w_gate=ffn_w_gate, ffn_w_up=ffn_w_up, ffn_w_down=ffn_w_down)
    return _trunk(x_prompt, x_sample, state_conv, w)
```

```python
import functools
import math

import jax
import jax.numpy as jnp
from jax import lax
from jax.experimental import pallas as pl
from jax.experimental.pallas import tpu as pltpu

D_MODEL = 1024
DEPTH = 4
SGU_CHUNK = 128
SGU_HEADS = 4
D_SGU = 3 * D_MODEL
SGU_HEAD_DIM = D_SGU // SGU_HEADS
CONV_WIDTH = 31
CONV_STATE = CONV_WIDTH - 1
D_FF = 2816
RMS_EPS = 1e-6
LN_EPS = 1e-5

LANES = 128
N_LANE_TILES = D_MODEL // LANES
HALO = 32
HALO_PAD = HALO - CONV_STATE
COL_CHUNK = 256
N_COL_CHUNKS = D_MODEL // COL_CHUNK
_SQUARE_CHUNKED = (N_COL_CHUNKS, D_MODEL, COL_CHUNK)
CONV_ROW_BLOCK = 64
VMEM_LIMIT_BYTES = 56 * 1024 * 1024

PROMPT_TILE = 512
PROMPT_TILE_FFN = 1024

F32 = jnp.float32
BF16 = jnp.bfloat16

_GELU_C = math.sqrt(2.0 / math.pi)


def _rms(x, g):
    return x * lax.rsqrt(jnp.mean(x * x, axis=-1, keepdims=True) + RMS_EPS) * g


def _gelu(z):
    w = (z * z) * (_GELU_C * 0.044715) + _GELU_C
    hz = 0.5 * z
    return hz + hz * jnp.tanh(z * w)


def _gelu_consts():
    return jnp.broadcast_to(jnp.array([[_GELU_C * 0.044715], [_GELU_C]], F32), (2, SGU_HEAD_DIM))


def _gelu_bf16(z, gk_ref):
    n = z.shape[1]
    zb = z.astype(BF16)
    w = (zb * zb) * gk_ref[0:1, 0:n].astype(BF16) + gk_ref[1:2, 0:n].astype(BF16)
    hz = zb * jnp.asarray(0.5, BF16)
    return hz + hz * jnp.tanh(zb * w)


def _dot(a, b):
    return jnp.dot(a, b, preferred_element_type=F32)


def _dot_cols(a, w_ref, k0=0, k1=None):
    return jnp.concatenate([_dot(a, w_ref[c, k0:k1, :]) for c in range(w_ref.shape[0])], axis=1)


def _store_cols(o_ref, wb):
    for c in range(o_ref.shape[0]):
        o_ref[c] = wb[:, c * COL_CHUNK:(c + 1) * COL_CHUNK]


def _lane_partial(v):
    acc = v[:, :LANES]
    for k in range(LANES, v.shape[1], LANES):
        acc = acc + v[:, k:k + LANES]
    return acc


def _layer_norm_stats(s1, s2, n):
    mean = jnp.sum(s1, axis=-1, keepdims=True) * (1.0 / n)
    var = jnp.sum(s2, axis=-1, keepdims=True) * (1.0 / n) - mean * mean
    return mean, lax.rsqrt(var + LN_EPS)


def _causal(ws, chunk_len):
    row = lax.broadcasted_iota(jnp.int32, (chunk_len, chunk_len), 0)
    col = lax.broadcasted_iota(jnp.int32, (chunk_len, chunk_len), 1)
    return jnp.where(row >= col, ws, 0.0).astype(BF16)


def _layer_spec(layer, shape):
    nd = len(shape)
    return pl.BlockSpec((None,) + tuple(shape), lambda i: (layer,) + (0,) * nd, pipeline_mode=pl.Buffered(1))


def _const_spec(shape):
    nd = len(shape)
    return pl.BlockSpec(shape, lambda i: (0,) * nd, pipeline_mode=pl.Buffered(1))


def _params():
    return pltpu.CompilerParams(dimension_semantics=("arbitrary",), vmem_limit_bytes=VMEM_LIMIT_BYTES)


def _ffn_chunk(h, wg_ref, wu_ref, act_ref, j):
    g = _dot(h, wg_ref[:, j:j + COL_CHUNK])
    u = _dot(h, wu_ref[:, j:j + COL_CHUNK])
    act_ref[:, j:j + COL_CHUNK] = (jax.nn.silu(g) * u).astype(BF16)


def _ffn_kernel(*refs, final):
    if final:
        x_ref, g_ref, wg_ref, wu_ref, wd_ref, gf_ref, o_ref, act_ref = refs
    else:
        x_ref, g_ref, wg_ref, wu_ref, wd_ref, o_ref, act_ref = refs
    x = x_ref[...]
    h = _rms(x, g_ref[...]).astype(BF16)
    for j in range(0, D_FF, COL_CHUNK):
        _ffn_chunk(h, wg_ref, wu_ref, act_ref, j)
    xo = x + _dot_cols(act_ref[...], wd_ref)
    if final:
        xo = _rms(xo, gf_ref[...])
    o_ref[...] = xo


def _ffn_specs(layer):
    specs = [_layer_spec(layer, (1, D_MODEL)), _const_spec((D_MODEL, D_FF)), _const_spec((D_MODEL, D_FF)),
             _const_spec((N_COL_CHUNKS, D_FF, COL_CHUNK))]
    if layer == DEPTH - 1:
        specs.append(_const_spec((1, D_MODEL)))
    return specs


def _ffn_args(w, wb, layer):
    args = [w["norm_ffn_g"], wb["gate"], wb["up"], wb["down"]]
    if layer == DEPTH - 1:
        args.append(w["norm_final_g"])
    return args


def _ffn(x, w, wb, layer, tile):
    rows = x.shape[0]
    final = layer == DEPTH - 1
    row_spec = pl.BlockSpec((tile, D_MODEL), lambda i: (i, 0))
    return pl.pallas_call(
        functools.partial(_ffn_kernel, final=final),
        out_shape=jax.ShapeDtypeStruct((rows, D_MODEL), F32),
        grid=(rows // tile,),
        in_specs=[row_spec] + _ffn_specs(layer),
        out_specs=row_spec,
        scratch_shapes=[pltpu.VMEM((tile, D_FF), BF16)],
        compiler_params=_params(),
        name="ffn_final" if final else "ffn",
    )(x, *_ffn_args(w, wb, layer))


def _ffn_sample_kernel(*refs, n_chunks, final):
    x_ref, g_ref, wg_ref, wu_ref, wd_ref = refs[:5]
    gf_ref = refs[5] if final else None
    o_ref, wgo_ref, wuo_ref, wdo_ref, h_scr, acc_scr = refs[-6:]
    j = pl.program_id(0)

    @pl.when(j == 0)
    def _():
        x = x_ref[...]
        h_scr[...] = _rms(x, g_ref[...]).astype(BF16)
        acc_scr[...] = x

    wg = wg_ref[...].astype(BF16)
    wu = wu_ref[...].astype(BF16)
    wd = wd_ref[...].astype(BF16)
    wgo_ref[...] = wg
    wuo_ref[...] = wu
    _store_cols(wdo_ref, wd)
    h = h_scr[...]
    act = (jax.nn.silu(_dot(h, wg)) * _dot(h, wu)).astype(BF16)
    acc_scr[...] += _dot(act, wd)

    @pl.when(j == n_chunks - 1)
    def _():
        y = acc_scr[...]
        if final:
            y = _rms(y, gf_ref[...])
        o_ref[...] = y


def _ffn_sample(x, w, layer):
    rows = x.shape[0]
    final = layer == DEPTH - 1
    n_chunks = D_FF // COL_CHUNK
    col_in = pl.BlockSpec((None, D_MODEL, COL_CHUNK), lambda j: (layer, 0, j))
    row_in = pl.BlockSpec((None, COL_CHUNK, D_MODEL), lambda j: (layer, j, 0))
    col_out = pl.BlockSpec((D_MODEL, COL_CHUNK), lambda j: (0, j))
    row_out = pl.BlockSpec((N_COL_CHUNKS, COL_CHUNK, COL_CHUNK), lambda j: (0, j, 0))
    in_specs = [_const_spec((rows, D_MODEL)), _layer_spec(layer, (1, D_MODEL)), col_in, col_in, row_in]
    args = [x, w["norm_ffn_g"], w["ffn_w_gate"], w["ffn_w_up"], w["ffn_w_down"]]
    if final:
        in_specs.append(_const_spec((1, D_MODEL)))
        args.append(w["norm_final_g"])
    y, gate, up, down = pl.pallas_call(
        functools.partial(_ffn_sample_kernel, n_chunks=n_chunks, final=final),
        out_shape=(jax.ShapeDtypeStruct((rows, D_MODEL), F32), jax.ShapeDtypeStruct((D_MODEL, D_FF), BF16),
                   jax.ShapeDtypeStruct((D_MODEL, D_FF), BF16), jax.ShapeDtypeStruct((N_COL_CHUNKS, D_FF, COL_CHUNK), BF16)),
        grid=(n_chunks,),
        in_specs=in_specs,
        out_specs=(_const_spec((rows, D_MODEL)), col_out, col_out, row_out),
        scratch_shapes=[pltpu.VMEM((rows, D_MODEL), BF16), pltpu.VMEM((rows, D_MODEL), F32)],
        compiler_params=_params(),
        name="ffn_sample_final" if final else "ffn_sample",
    )(*args)
    return y, dict(gate=gate, up=up, down=down)


def _sgu_kernel(x_ref, g_ref, win_ref, bin_ref, lng_ref, lnb_ref, ws_ref, bsb_ref, wout_ref, bout_ref, gk_ref,
                o_ref, u_scr, v_scr, vn_scr, gated_scr):
    tile = x_ref.shape[0]
    h = _rms(x_ref[...], g_ref[...]).astype(BF16)

    s1 = jnp.zeros((tile, LANES), F32)
    s2 = jnp.zeros((tile, LANES), F32)
    for j in range(0, D_SGU, COL_CHUNK):
        z = _dot(h, win_ref[(D_SGU + j) // COL_CHUNK]) + bin_ref[:, D_SGU + j:D_SGU + j + COL_CHUNK]
        vb = _gelu_bf16(z, gk_ref)
        v = vb.astype(F32)
        s1 = s1 + _lane_partial(v)
        s2 = s2 + _lane_partial(v * v)
        v_scr[:, j:j + COL_CHUNK] = vb
    mean, rstd = _layer_norm_stats(s1, s2, D_SGU)
    for j in range(0, D_SGU, COL_CHUNK):
        u_scr[:, j:j + COL_CHUNK] = (_dot(h, win_ref[j // COL_CHUNK]) + bin_ref[:, j:j + COL_CHUNK]).astype(BF16)
        vn = (v_scr[:, j:j + COL_CHUNK] - mean) * rstd * lng_ref[:, j:j + COL_CHUNK] + lnb_ref[:, j:j + COL_CHUNK]
        vn_scr[:, j:j + COL_CHUNK] = vn.astype(BF16)

    o_ref[...] = x_ref[...] + bout_ref[...]
    for hd in range(SGU_HEADS):
        w = _causal(ws_ref[hd], SGU_CHUNK)
        bias = jnp.concatenate([bsb_ref[hd]] * (SGU_HEAD_DIM // LANES), axis=1)
        c0 = hd * SGU_HEAD_DIM
        for r0 in range(0, tile, SGU_CHUNK):
            mixed = _dot(w, vn_scr[r0:r0 + SGU_CHUNK, c0:c0 + SGU_HEAD_DIM]) + bias
            gated_scr[r0:r0 + SGU_CHUNK, c0:c0 + SGU_HEAD_DIM] = (
                _gelu_bf16(u_scr[r0:r0 + SGU_CHUNK, c0:c0 + SGU_HEAD_DIM], gk_ref) * mixed.astype(BF16))
        o_ref[...] += _dot_cols(gated_scr[:, c0:c0 + SGU_HEAD_DIM], wout_ref, c0, c0 + SGU_HEAD_DIM)


def _sgu(x, w, wb, layer, tile):
    rows = x.shape[0]
    j = layer // 2
    row_spec = pl.BlockSpec((tile, D_MODEL), lambda i: (i, 0))
    in_specs = [row_spec, _layer_spec(layer, (1, D_MODEL)), _const_spec((2 * D_SGU // COL_CHUNK, D_MODEL, COL_CHUNK)),
                _layer_spec(j, (1, 2 * D_SGU)), _layer_spec(j, (1, D_SGU)), _layer_spec(j, (1, D_SGU)),
                _layer_spec(j, (SGU_HEADS, SGU_CHUNK, SGU_CHUNK)), _layer_spec(j, (SGU_HEADS, SGU_CHUNK, LANES)),
                _const_spec((N_COL_CHUNKS, D_SGU, COL_CHUNK)), _layer_spec(j, (1, D_MODEL)),
                _const_spec((2, SGU_HEAD_DIM))]
    bsb = jnp.broadcast_to(w["sgu_b_s"][:, :, :, None], w["sgu_b_s"].shape + (LANES,))
    return pl.pallas_call(
        _sgu_kernel,
        out_shape=jax.ShapeDtypeStruct((rows, D_MODEL), F32),
        grid=(rows // tile,),
        in_specs=in_specs,
        out_specs=row_spec,
        scratch_shapes=[pltpu.VMEM((tile, D_SGU), BF16)] * 4,
        compiler_params=_params(),
        name="sgu",
    )(x, w["norm_mix_g"], wb["w_in"], w["sgu_b_in"], w["sgu_ln_g"], w["sgu_ln_b"], w["sgu_w_s"], bsb,
      wb["w_out"], w["sgu_b_out"], _gelu_consts())


def _sgu_sample_kernel(x_ref, g_ref, win_ref, bin_ref, lng_ref, lnb_ref, ws_ref, bsb_ref, wout_ref, bout_ref,
                       o_ref, v_ref, wino_ref, wouto_ref,
                       h_scr, u_scr, v_scr, s1_scr, s2_scr, gated_scr, acc_scr, *, chunk_len):
    s = pl.program_id(0)
    rows = x_ref.shape[0]
    nh = SGU_HEADS

    @pl.when(s == 0)
    def _():
        h_scr[...] = _rms(x_ref[...], g_ref[...]).astype(BF16)
        s1_scr[...] = jnp.zeros(s1_scr.shape, F32)
        s2_scr[...] = jnp.zeros(s2_scr.shape, F32)

    @pl.when(s < 2 * nh)
    def _():
        wb = win_ref[...].astype(BF16)
        _store_cols(wino_ref, wb)
        z = _dot(h_scr[...], wb) + bin_ref[...]

        @pl.when(s < nh)
        def _():
            v = _gelu(z)
            s1_scr[...] += _lane_partial(v)
            s2_scr[...] += _lane_partial(v * v)
            v_scr[s] = v

        @pl.when(s >= nh)
        def _():
            u_scr[s - nh] = z

    @pl.when(s == 2 * nh)
    def _():
        mean, rstd = _layer_norm_stats(s1_scr[...], s2_scr[...], D_SGU)
        for hd in range(nh):
            c0 = hd * SGU_HEAD_DIM
            vn = (v_scr[hd] - mean) * rstd * lng_ref[:, c0:c0 + SGU_HEAD_DIM] + lnb_ref[:, c0:c0 + SGU_HEAD_DIM]
            v_ref[:, c0:c0 + SGU_HEAD_DIM] = vn
            vnb = vn.astype(BF16)
            w = _causal(ws_ref[hd], chunk_len)
            bias = jnp.concatenate([bsb_ref[hd]] * (SGU_HEAD_DIM // LANES), axis=1)
            for r0 in range(0, rows, chunk_len):
                mixed = _dot(w, vnb[r0:r0 + chunk_len]) + bias
                gated_scr[hd, r0:r0 + chunk_len, :] = (_gelu(u_scr[hd, r0:r0 + chunk_len, :]) * mixed).astype(BF16)
        acc_scr[...] = x_ref[...] + bout_ref[...]

    @pl.when(s >= 2 * nh)
    def _():
        wb = wout_ref[...].astype(BF16)
        _store_cols(wouto_ref, wb)
        acc_scr[...] += _dot(gated_scr[s - 2 * nh], wb)

    @pl.when(s == 3 * nh - 1)
    def _():
        o_ref[...] = acc_scr[...]


def _sgu_sample(x, w, layer, chunk_len):
    rows = x.shape[0]
    j = layer // 2
    nh = SGU_HEADS
    hw = SGU_HEAD_DIM
    in_col = lambda s: (jnp.minimum(s, 2 * nh - 1) + nh) % (2 * nh)
    out_row = lambda s: jnp.maximum(s - 2 * nh, 0)
    ws = w["sgu_w_s"][:, :, :chunk_len, :chunk_len]
    bsb = jnp.broadcast_to(w["sgu_b_s"][:, :, :chunk_len, None], ws.shape[:3] + (LANES,))
    in_specs = [_const_spec((rows, D_MODEL)), _layer_spec(layer, (1, D_MODEL)),
                pl.BlockSpec((None, D_MODEL, hw), lambda s: (j, 0, in_col(s))),
                pl.BlockSpec((None, 1, hw), lambda s: (j, 0, in_col(s))),
                _layer_spec(j, (1, D_SGU)), _layer_spec(j, (1, D_SGU)),
                _layer_spec(j, (nh, chunk_len, chunk_len)), _layer_spec(j, (nh, chunk_len, LANES)),
                pl.BlockSpec((None, hw, D_MODEL), lambda s: (j, out_row(s), 0)),
                _layer_spec(j, (1, D_MODEL))]
    y, v, w_in, w_out = pl.pallas_call(
        functools.partial(_sgu_sample_kernel, chunk_len=chunk_len),
        out_shape=(jax.ShapeDtypeStruct((rows, D_MODEL), F32), jax.ShapeDtypeStruct((rows, D_SGU), F32),
                   jax.ShapeDtypeStruct((2 * D_SGU // COL_CHUNK, D_MODEL, COL_CHUNK), BF16),
                   jax.ShapeDtypeStruct((N_COL_CHUNKS, D_SGU, COL_CHUNK), BF16)),
        grid=(3 * nh,),
        in_specs=in_specs,
        out_specs=(_const_spec((rows, D_MODEL)), _const_spec((rows, D_SGU)),
                   pl.BlockSpec((hw // COL_CHUNK, D_MODEL, COL_CHUNK), lambda s: (in_col(s), 0, 0)),
                   pl.BlockSpec((N_COL_CHUNKS, hw, COL_CHUNK), lambda s: (0, out_row(s), 0))),
        scratch_shapes=[pltpu.VMEM((rows, D_MODEL), BF16), pltpu.VMEM((nh, rows, hw), F32),
                        pltpu.VMEM((nh, rows, hw), F32), pltpu.VMEM((rows, LANES), F32),
                        pltpu.VMEM((rows, LANES), F32), pltpu.VMEM((nh, rows, hw), BF16),
                        pltpu.VMEM((rows, D_MODEL), F32)],
        compiler_params=_params(),
        name="sgu_sample",
    )(x, w["norm_mix_g"], w["sgu_w_in"], w["sgu_b_in"], w["sgu_ln_g"], w["sgu_ln_b"], ws, bsb,
      w["sgu_w_out"], w["sgu_b_out"])
    return y, v, dict(w_in=w_in, w_out=w_out)


def _depthwise(pad_scr, conv_scr, wdw_ref, bdw_ref, s, l, r0, out_row, row_block):
    acc = jnp.broadcast_to(bdw_ref[l], (row_block, LANES))
    for k in range(CONV_WIDTH):
        acc = acc + wdw_ref[l, k:k + 1, :] * pad_scr[s, l, r0 + k + HALO_PAD:r0 + k + HALO_PAD + row_block, :]
    conv_scr[l, out_row:out_row + row_block, :] = acc


def _conv_norm_act(conv_scr, lng_ref, lnb_ref):
    c = jnp.concatenate([conv_scr[l] for l in range(N_LANE_TILES)], axis=1)
    mu = jnp.mean(c, axis=-1, keepdims=True)
    cc = c - mu
    var = jnp.mean(cc * cc, axis=-1, keepdims=True)
    cn = cc * lax.rsqrt(var + LN_EPS) * lng_ref[...] + lnb_ref[...]
    return jax.nn.silu(cn).astype(BF16)


def _conv_state(pad_scr, s, seg_len):
    return jnp.concatenate(
        [pad_scr[s, l, HALO + seg_len - CONV_STATE:HALO + seg_len, :] for l in range(N_LANE_TILES)], axis=1)


def _conv_mixer(x, g_ref, w1a_ref, w1b_ref, b1_ref, wdw_ref, bdw_ref, lng_ref, lnb_ref, w2_ref, b2_ref, st_ref,
                pad_scr, conv_scr, *, between):
    seg_len = x.shape[0]
    h = _rms(x, g_ref[...]).astype(BF16)
    for j in range(0, D_MODEL, COL_CHUNK):
        a1 = _dot(h, w1a_ref[j // COL_CHUNK]) + b1_ref[:, j:j + COL_CHUNK]
        a2 = _dot(h, w1b_ref[j // COL_CHUNK]) + b1_ref[:, D_MODEL + j:D_MODEL + j + COL_CHUNK]
        glu = a1 * jax.nn.sigmoid(a2)
        for jj in range(0, COL_CHUNK, LANES):
            pad_scr[0, (j + jj) // LANES, HALO:HALO + seg_len, :] = glu[:, jj:jj + LANES]
        for l in range(j // LANES, (j + COL_CHUNK) // LANES):
            for r0 in range(0, seg_len, CONV_ROW_BLOCK):
                _depthwise(pad_scr, conv_scr, wdw_ref, bdw_ref, 0, l, r0, r0, CONV_ROW_BLOCK)
                between(l, r0)

    st_ref[0] = _conv_state(pad_scr, 0, seg_len)
    pad_scr[:, :, 0:HALO, :] = pad_scr[:, :, seg_len:seg_len + HALO, :]
    act = _conv_norm_act(conv_scr, lng_ref, lnb_ref)
    return x + _dot_cols(act, w2_ref) + b2_ref[...]


def _conv_sample_kernel(x_ref, g_ref, w1a_ref, w1b_ref, b1a_ref, b1b_ref, past_ref, wdw_ref, bdw_ref, lng_ref, lnb_ref,
                        w2_ref, b2_ref, o_ref, st_ref, w1ao_ref, w1bo_ref, w2o_ref,
                        h_scr, pad_scr, conv_scr, act_scr, acc_scr, *, nseg, seg_len):
    s = pl.program_id(0)
    n = D_MODEL // COL_CHUNK

    @pl.when(s == 0)
    def _():
        h_scr[...] = _rms(x_ref[...], g_ref[...]).astype(BF16)
        for l in range(N_LANE_TILES):
            pad_scr[:, l, HALO_PAD:HALO, :] = past_ref[:, :, l * LANES:(l + 1) * LANES]

    @pl.when(s < n)
    def _():
        wa = w1a_ref[...].astype(BF16)
        wb = w1b_ref[...].astype(BF16)
        w1ao_ref[0] = wa
        w1bo_ref[0] = wb
        h = h_scr[...]
        glu = (_dot(h, wa) + b1a_ref[...]) * jax.nn.sigmoid(_dot(h, wb) + b1b_ref[...])
        for jj in range(COL_CHUNK // LANES):
            l = s * (COL_CHUNK // LANES) + jj
            for q in range(nseg):
                pad_scr[q, l, HALO:HALO + seg_len, :] = glu[q * seg_len:(q + 1) * seg_len, jj * LANES:(jj + 1) * LANES]
            for q in range(nseg):
                _depthwise(pad_scr, conv_scr, wdw_ref, bdw_ref, q, l, 0, q * seg_len, seg_len)

    @pl.when(s == n)
    def _():
        for q in range(nseg):
            st_ref[q] = _conv_state(pad_scr, q, seg_len)
        act = _conv_norm_act(conv_scr, lng_ref, lnb_ref)
        for r in range(n):
            act_scr[r] = act[:, r * COL_CHUNK:(r + 1) * COL_CHUNK]
        acc_scr[...] = x_ref[...] + b2_ref[...]

    @pl.when(s >= n)
    def _():
        wb = w2_ref[...].astype(BF16)
        _store_cols(w2o_ref, wb)
        acc_scr[...] += _dot(act_scr[s - n], wb)

    @pl.when(s == 2 * n - 1)
    def _():
        o_ref[...] = acc_scr[...]


def _conv_sample(x, state, w, layer):
    rows = x.shape[0]
    nseg = state.shape[1]
    seg_len = rows // nseg
    j = layer // 2
    st_shape = state.shape[1:]
    n = D_MODEL // COL_CHUNK
    col = lambda s: jnp.minimum(s, n - 1)
    krow = lambda s: jnp.maximum(s - n, 0)
    in_specs = [_const_spec((rows, D_MODEL)), _layer_spec(layer, (1, D_MODEL)),
                pl.BlockSpec((None, D_MODEL, COL_CHUNK), lambda s: (j, 0, col(s))),
                pl.BlockSpec((None, D_MODEL, COL_CHUNK), lambda s: (j, 0, n + col(s))),
                pl.BlockSpec((None, 1, COL_CHUNK), lambda s: (j, 0, col(s))),
                pl.BlockSpec((None, 1, COL_CHUNK), lambda s: (j, 0, n + col(s))),
                _layer_spec(j, st_shape),
                _layer_spec(j, (N_LANE_TILES, CONV_WIDTH, LANES)), _layer_spec(j, (N_LANE_TILES, 1, LANES)),
                _layer_spec(j, (1, D_MODEL)), _layer_spec(j, (1, D_MODEL)),
                pl.BlockSpec((None, COL_CHUNK, D_MODEL), lambda s: (j, krow(s), 0)),
                _layer_spec(j, (1, D_MODEL))]
    y, st, w1a, w1b, w2 = pl.pallas_call(
        functools.partial(_conv_sample_kernel, nseg=nseg, seg_len=seg_len),
        out_shape=(jax.ShapeDtypeStruct((rows, D_MODEL), F32), jax.ShapeDtypeStruct(st_shape, F32),
                   jax.ShapeDtypeStruct(_SQUARE_CHUNKED, BF16), jax.ShapeDtypeStruct(_SQUARE_CHUNKED, BF16),
                   jax.ShapeDtypeStruct(_SQUARE_CHUNKED, BF16)),
        grid=(2 * n,),
        in_specs=in_specs,
        out_specs=(_const_spec((rows, D_MODEL)), _const_spec(st_shape),
                   pl.BlockSpec((1, D_MODEL, COL_CHUNK), lambda s: (col(s), 0, 0)),
                   pl.BlockSpec((1, D_MODEL, COL_CHUNK), lambda s: (col(s), 0, 0)),
                   pl.BlockSpec((N_COL_CHUNKS, COL_CHUNK, COL_CHUNK), lambda s: (0, krow(s), 0))),
        scratch_shapes=[pltpu.VMEM((rows, D_MODEL), BF16),
                        pltpu.VMEM((nseg, N_LANE_TILES, HALO + seg_len, LANES), F32),
                        pltpu.VMEM((N_LANE_TILES, rows, LANES), F32),
                        pltpu.VMEM((n, rows, COL_CHUNK), BF16), pltpu.VMEM((rows, D_MODEL), F32)],
        compiler_params=_params(),
        name="conv_sample",
    )(x, w["norm_mix_g"], w["conv_w_pw1"], w["conv_w_pw1"], w["conv_b_pw1"], w["conv_b_pw1"], state,
      w["conv_w_dw"], w["conv_b_dw"], w["conv_ln_g"], w["conv_ln_b"], w["conv_w_pw2"], w["conv_b_pw2"])
    return y, st, dict(w1a=w1a, w1b=w1b, w2=w2)


_FFN_SLOTS = ((0, 128), (0, 384), (1, 128), (2, 128), (2, 384), (3, 128), (4, 128), (4, 384), (5, 128),
              (6, 128), (7, 128))
assert len(_FFN_SLOTS) == D_FF // COL_CHUNK


def _conv_ffn_kernel(*refs, n_tiles, tiles_per_seq, final):
    x_ref = refs[0]
    conv_params, (gf_ref, wg_ref, wu_ref, wd_ref) = refs[1:11], refs[11:15]
    rest = refs[15:]
    if final:
        gfin_ref, rest = rest[0], rest[1:]
    o_ref, st_ref, pad_scr, conv_scr, xmid_scr, xprev_scr, act_scr = rest
    i = pl.program_id(0)

    @pl.when(i == 0)
    def _():
        xmid_scr[...] = jnp.zeros(xmid_scr.shape, F32)

    xprev_scr[...] = xmid_scr[...]

    @pl.when(jnp.minimum(i, n_tiles - 1) % tiles_per_seq == 0)
    def _():
        pad_scr[:, :, 0:HALO, :] = jnp.zeros((1, N_LANE_TILES, HALO, LANES), F32)

    xp = xprev_scr[...]
    hf = _rms(xp, gf_ref[...]).astype(BF16)

    def between(l, r0):
        if (l, r0) in _FFN_SLOTS:
            _ffn_chunk(hf, wg_ref, wu_ref, act_scr, _FFN_SLOTS.index((l, r0)) * COL_CHUNK)

    xmid_scr[...] = _conv_mixer(x_ref[...], *conv_params, st_ref, pad_scr, conv_scr, between=between)
    y = xp + _dot_cols(act_scr[...], wd_ref)
    if final:
        y = _rms(y, gfin_ref[...])
    o_ref[...] = y


def _conv_ffn(x, w, wb_conv, wb_ffn, layer, tile, seq_len):
    rows = x.shape[0]
    n_tiles = rows // tile
    tiles_per_seq = seq_len // tile
    final = layer == DEPTH - 1
    j = layer // 2
    sq = _SQUARE_CHUNKED
    conv_specs = [_layer_spec(layer, (1, D_MODEL)), _const_spec(sq), _const_spec(sq), _layer_spec(j, (1, 2 * D_MODEL)),
                  _layer_spec(j, (N_LANE_TILES, CONV_WIDTH, LANES)), _layer_spec(j, (N_LANE_TILES, 1, LANES)),
                  _layer_spec(j, (1, D_MODEL)), _layer_spec(j, (1, D_MODEL)), _const_spec(sq),
                  _layer_spec(j, (1, D_MODEL))]
    conv_args = [w["norm_mix_g"], wb_conv["w1a"], wb_conv["w1b"], w["conv_b_pw1"], w["conv_w_dw"], w["conv_b_dw"],
                 w["conv_ln_g"], w["conv_ln_b"], wb_conv["w2"], w["conv_b_pw2"]]
    return pl.pallas_call(
        functools.partial(_conv_ffn_kernel, n_tiles=n_tiles, tiles_per_seq=tiles_per_seq, final=final),
        out_shape=(jax.ShapeDtypeStruct((rows, D_MODEL), F32),
                   jax.ShapeDtypeStruct((rows // seq_len, CONV_STATE, D_MODEL), F32)),
        grid=(n_tiles + 1,),
        in_specs=[pl.BlockSpec((tile, D_MODEL), lambda i: (jnp.minimum(i, n_tiles - 1), 0))]
        + conv_specs + _ffn_specs(layer),
        out_specs=(pl.BlockSpec((tile, D_MODEL), lambda i: (jnp.maximum(i - 1, 0), 0)),
                   pl.BlockSpec((1, CONV_STATE, D_MODEL),
                                lambda i: (jnp.minimum(i, n_tiles - 1) // tiles_per_seq, 0, 0))),
        scratch_shapes=[pltpu.VMEM((1, N_LANE_TILES, HALO + tile, LANES), F32),
                        pltpu.VMEM((N_LANE_TILES, tile, LANES), F32),
                        pltpu.VMEM((tile, D_MODEL), F32), pltpu.VMEM((tile, D_MODEL), F32),
                        pltpu.VMEM((tile, D_FF), BF16)],
        compiler_params=_params(),
        name="conv_ffn_final" if final else "conv_ffn",
    )(x, *conv_args, *_ffn_args(w, wb_ffn, layer))


def _lane_tile_major(a):
    return a.reshape(a.shape[0], a.shape[1], N_LANE_TILES, LANES).transpose(0, 2, 1, 3)


def _trunk(x_prompt, x_sample, state_conv, w):
    nb_p, t_p, _ = x_prompt.shape
    nb_s, t_s, _ = x_sample.shape
    xp = x_prompt.reshape(nb_p * t_p, D_MODEL)
    xs = x_sample.reshape(nb_s * t_s, D_MODEL)
    conv_p, conv_s, sgu_vs = [], [], []
    for layer in range(DEPTH):
        if layer % 2 == 0:
            xs, v, wb_mix = _sgu_sample(xs, w, layer, t_s)
            sgu_vs.append(v.reshape(nb_s, t_s, D_SGU))
            xs, wb_ffn = _ffn_sample(xs, w, layer)
            xp = _sgu(xp, w, wb_mix, layer, PROMPT_TILE)
            xp = _ffn(xp, w, wb_ffn, layer, PROMPT_TILE_FFN)
        else:
            xs, st_s, wb_mix = _conv_sample(xs, state_conv, w, layer)
            xs, wb_ffn = _ffn_sample(xs, w, layer)
            xp, st_p = _conv_ffn(xp, w, wb_mix, wb_ffn, layer, PROMPT_TILE, t_p)
            conv_p.append(st_p)
            conv_s.append(st_s)
    return (xp.reshape(nb_p, t_p, D_MODEL), xs.reshape(nb_s, t_s, D_MODEL), jnp.stack(conv_p), jnp.stack(conv_s),
            jnp.stack(sgu_vs))


def kernel(x_prompt, x_sample, state_conv, norm_mix_g, norm_ffn_g, norm_final_g, sgu_w_in, sgu_b_in, sgu_ln_g, sgu_ln_b, sgu_w_s, sgu_b_s, sgu_w_out, sgu_b_out, conv_w_pw1, conv_b_pw1, conv_w_dw, conv_b_dw, conv_ln_g, conv_ln_b, conv_w_pw2, conv_b_pw2, ffn_w_gate, ffn_w_up, ffn_w_down):
    assert CONV_STATE <= x_sample.shape[1] <= SGU_CHUNK and x_prompt.shape[1] % PROMPT_TILE == 0
    assert (x_prompt.shape[0] * x_prompt.shape[1]) % PROMPT_TILE_FFN == 0
    row = lambda a: a.reshape(a.shape[0], 1, a.shape[1])
    w = dict(norm_mix_g=row(norm_mix_g), norm_ffn_g=row(norm_ffn_g), norm_final_g=norm_final_g.reshape(1, D_MODEL),
             sgu_w_in=sgu_w_in, sgu_b_in=row(sgu_b_in), sgu_ln_g=row(sgu_ln_g), sgu_ln_b=row(sgu_ln_b),
             sgu_w_s=sgu_w_s, sgu_b_s=sgu_b_s, sgu_w_out=sgu_w_out, sgu_b_out=row(sgu_b_out),
             conv_w_pw1=conv_w_pw1, conv_b_pw1=row(conv_b_pw1), conv_w_dw=_lane_tile_major(conv_w_dw),
             conv_b_dw=_lane_tile_major(row(conv_b_dw)), conv_ln_g=row(conv_ln_g), conv_ln_b=row(conv_ln_b),
             conv_w_pw2=conv_w_pw2, conv_b_pw2=row(conv_b_pw2),
             ffn_w_gate=ffn_w_gate, ffn_w_up=ffn_w_up, ffn_w_down=ffn_w_down)
    return _trunk(x_prompt, x_sample, state_conv, w)
```

```python
import functools
import math

import jax
import jax.numpy as jnp
from jax import lax
from jax.experimental import pallas as pl
from jax.experimental.pallas import tpu as pltpu

D_MODEL = 1024
DEPTH = 4
SGU_CHUNK = 128
SGU_HEADS = 4
D_SGU = 3 * D_MODEL
SGU_HEAD_DIM = D_SGU // SGU_HEADS
CONV_WIDTH = 31
CONV_STATE = CONV_WIDTH - 1
D_FF = 2816
RMS_EPS = 1e-6
LN_EPS = 1e-5

LANES = 128
N_LANE_TILES = D_MODEL // LANES
HALO = 32
HALO_PAD = HALO - CONV_STATE
COL_CHUNK = 256
N_COL_CHUNKS = D_MODEL // COL_CHUNK
_SQUARE_CHUNKED = (N_COL_CHUNKS, D_MODEL, COL_CHUNK)
CONV_ROW_BLOCK = 64
VMEM_LIMIT_BYTES = 56 * 1024 * 1024

PROMPT_TILE = 512
PROMPT_TILE_FFN = 1024

F32 = jnp.float32
BF16 = jnp.bfloat16

_GELU_C = math.sqrt(2.0 / math.pi)


def _rms(x, g):
    return x * lax.rsqrt(jnp.mean(x * x, axis=-1, keepdims=True) + RMS_EPS) * g


def _gelu(z):
    w = (z * z) * (_GELU_C * 0.044715) + _GELU_C
    hz = 0.5 * z
    return hz + hz * jnp.tanh(z * w)


def _gelu_consts():
    return jnp.broadcast_to(jnp.array([[_GELU_C * 0.044715], [_GELU_C]], F32), (2, SGU_HEAD_DIM))


def _gelu_bf16(z, gk_ref):
    n = z.shape[1]
    zb = z.astype(BF16)
    w = (zb * zb) * gk_ref[0:1, 0:n].astype(BF16) + gk_ref[1:2, 0:n].astype(BF16)
    hz = zb * jnp.asarray(0.5, BF16)
    return hz + hz * jnp.tanh(zb * w)


def _dot(a, b):
    return jnp.dot(a, b, preferred_element_type=F32)


def _dot_cols(a, w_ref, k0=0, k1=None):
    return jnp.concatenate([_dot(a, w_ref[c, k0:k1, :]) for c in range(w_ref.shape[0])], axis=1)


def _store_cols(o_ref, wb):
    for c in range(o_ref.shape[0]):
        o_ref[c] = wb[:, c * COL_CHUNK:(c + 1) * COL_CHUNK]


def _lane_partial(v):
    acc = v[:, :LANES]
    for k in range(LANES, v.shape[1], LANES):
        acc = acc + v[:, k:k + LANES]
    return acc


def _layer_norm_stats(s1, s2, n):
    mean = jnp.sum(s1, axis=-1, keepdims=True) * (1.0 / n)
    var = jnp.sum(s2, axis=-1, keepdims=True) * (1.0 / n) - mean * mean
    return mean, lax.rsqrt(var + LN_EPS)


def _causal(ws, chunk_len):
    row = lax.broadcasted_iota(jnp.int32, (chunk_len, chunk_len), 0)
    col = lax.broadcasted_iota(jnp.int32, (chunk_len, chunk_len), 1)
    return jnp.where(row >= col, ws, 0.0).astype(BF16)


def _layer_spec(layer, shape):
    nd = len(shape)
    return pl.BlockSpec((None,) + tuple(shape), lambda i: (layer,) + (0,) * nd, pipeline_mode=pl.Buffered(1))


def _const_spec(shape):
    nd = len(shape)
    return pl.BlockSpec(shape, lambda i: (0,) * nd, pipeline_mode=pl.Buffered(1))


_STREAM = None


def _params():
    return pltpu.CompilerParams(dimension_semantics=("arbitrary",), vmem_limit_bytes=VMEM_LIMIT_BYTES)


def _ffn_chunk(h, wg_ref, wu_ref, act_ref, j, narrow=False):
    g = _dot(h, wg_ref[:, j:j + COL_CHUNK])
    u = _dot(h, wu_ref[:, j:j + COL_CHUNK])
    if narrow:
        gb = g.astype(BF16)
        act_ref[:, j:j + COL_CHUNK] = gb * jax.nn.sigmoid(gb) * u.astype(BF16)
    else:
        act_ref[:, j:j + COL_CHUNK] = (jax.nn.silu(g) * u).astype(BF16)


def _ffn_kernel(*refs, final):
    if final:
        x_ref, g_ref, wg_ref, wu_ref, wd_ref, gf_ref, o_ref, act_ref = refs
    else:
        x_ref, g_ref, wg_ref, wu_ref, wd_ref, o_ref, act_ref = refs
    x = x_ref[...]
    h = _rms(x, g_ref[...]).astype(BF16)
    for j in range(0, D_FF, COL_CHUNK):
        _ffn_chunk(h, wg_ref, wu_ref, act_ref, j)
    xo = x + _dot_cols(act_ref[...], wd_ref)
    if final:
        xo = _rms(xo, gf_ref[...])
    o_ref[...] = xo


def _ffn_specs(layer):
    specs = [_layer_spec(layer, (1, D_MODEL)), _const_spec((D_MODEL, D_FF)), _const_spec((D_MODEL, D_FF)),
             _const_spec((N_COL_CHUNKS, D_FF, COL_CHUNK))]
    if layer == DEPTH - 1:
        specs.append(_const_spec((1, D_MODEL)))
    return specs


def _ffn_args(w, wb, layer):
    args = [w["norm_ffn_g"], wb["gate"], wb["up"], wb["down"]]
    if layer == DEPTH - 1:
        args.append(w["norm_final_g"])
    return args


def _ffn(x, w, wb, layer, tile):
    rows = x.shape[0]
    final = layer == DEPTH - 1
    row_spec = pl.BlockSpec((tile, D_MODEL), lambda i: (i, 0))
    return pl.pallas_call(
        functools.partial(_ffn_kernel, final=final),
        out_shape=jax.ShapeDtypeStruct((rows, D_MODEL), F32),
        grid=(rows // tile,),
        in_specs=[row_spec] + _ffn_specs(layer),
        out_specs=row_spec,
        scratch_shapes=[pltpu.VMEM((tile, D_FF), BF16)],
        compiler_params=_params(),
        name="ffn_final" if final else "ffn",
    )(x, *_ffn_args(w, wb, layer))


def _ffn_sample_kernel(*refs, n_chunks, final):
    x_ref, g_ref, wg_ref, wu_ref, wd_ref = refs[:5]
    gf_ref = refs[5] if final else None
    o_ref, wgo_ref, wuo_ref, wdo_ref, h_scr, acc_scr = refs[-6:]
    j = pl.program_id(0)

    @pl.when(j == 0)
    def _():
        x = x_ref[...]
        h_scr[...] = _rms(x, g_ref[...]).astype(BF16)
        acc_scr[...] = x

    wg = wg_ref[...].astype(BF16)
    wu = wu_ref[...].astype(BF16)
    wd = wd_ref[...].astype(BF16)
    wgo_ref[...] = wg
    wuo_ref[...] = wu
    _store_cols(wdo_ref, wd)
    h = h_scr[...]
    act = (jax.nn.silu(_dot(h, wg)) * _dot(h, wu)).astype(BF16)
    acc_scr[...] += _dot(act, wd)

    @pl.when(j == n_chunks - 1)
    def _():
        y = acc_scr[...]
        if final:
            y = _rms(y, gf_ref[...])
        o_ref[...] = y


def _ffn_sample(x, w, layer):
    rows = x.shape[0]
    final = layer == DEPTH - 1
    n_chunks = D_FF // COL_CHUNK
    col_in = pl.BlockSpec((None, D_MODEL, COL_CHUNK), lambda j: (layer, 0, j), pipeline_mode=_STREAM)
    row_in = pl.BlockSpec((None, COL_CHUNK, D_MODEL), lambda j: (layer, j, 0), pipeline_mode=_STREAM)
    col_out = pl.BlockSpec((D_MODEL, COL_CHUNK), lambda j: (0, j))
    row_out = pl.BlockSpec((N_COL_CHUNKS, COL_CHUNK, COL_CHUNK), lambda j: (0, j, 0))
    in_specs = [_const_spec((rows, D_MODEL)), _layer_spec(layer, (1, D_MODEL)), col_in, col_in, row_in]
    args = [x, w["norm_ffn_g"], w["ffn_w_gate"], w["ffn_w_up"], w["ffn_w_down"]]
    if final:
        in_specs.append(_const_spec((1, D_MODEL)))
        args.append(w["norm_final_g"])
    y, gate, up, down = pl.pallas_call(
        functools.partial(_ffn_sample_kernel, n_chunks=n_chunks, final=final),
        out_shape=(jax.ShapeDtypeStruct((rows, D_MODEL), F32), jax.ShapeDtypeStruct((D_MODEL, D_FF), BF16),
                   jax.ShapeDtypeStruct((D_MODEL, D_FF), BF16), jax.ShapeDtypeStruct((N_COL_CHUNKS, D_FF, COL_CHUNK), BF16)),
        grid=(n_chunks,),
        in_specs=in_specs,
        out_specs=(_const_spec((rows, D_MODEL)), col_out, col_out, row_out),
        scratch_shapes=[pltpu.VMEM((rows, D_MODEL), BF16), pltpu.VMEM((rows, D_MODEL), F32)],
        compiler_params=_params(),
        name="ffn_sample_final" if final else "ffn_sample",
    )(*args)
    return y, dict(gate=gate, up=up, down=down)


def _sgu_kernel(x_ref, g_ref, win_ref, bin_ref, lng_ref, lnb_ref, ws_ref, bsb_ref, wout_ref, bout_ref, gk_ref,
                o_ref, u_scr, v_scr, vn_scr, gated_scr):
    tile = x_ref.shape[0]
    h = _rms(x_ref[...], g_ref[...]).astype(BF16)

    s1 = jnp.zeros((tile, LANES), F32)
    s2 = jnp.zeros((tile, LANES), F32)
    for j in range(0, D_SGU, COL_CHUNK):
        z = _dot(h, win_ref[(D_SGU + j) // COL_CHUNK]) + bin_ref[:, D_SGU + j:D_SGU + j + COL_CHUNK]
        vb = _gelu_bf16(z, gk_ref)
        v = vb.astype(F32)
        s1 = s1 + _lane_partial(v)
        s2 = s2 + _lane_partial(v * v)
        v_scr[:, j:j + COL_CHUNK] = vb
    mean, rstd = _layer_norm_stats(s1, s2, D_SGU)
    for j in range(0, D_SGU, COL_CHUNK):
        u_scr[:, j:j + COL_CHUNK] = (_dot(h, win_ref[j // COL_CHUNK]) + bin_ref[:, j:j + COL_CHUNK]).astype(BF16)
        vn = (v_scr[:, j:j + COL_CHUNK] - mean) * rstd * lng_ref[:, j:j + COL_CHUNK] + lnb_ref[:, j:j + COL_CHUNK]
        vn_scr[:, j:j + COL_CHUNK] = vn.astype(BF16)

    o_ref[...] = x_ref[...] + bout_ref[...]
    for hd in range(SGU_HEADS):
        w = _causal(ws_ref[hd], SGU_CHUNK)
        bias = jnp.concatenate([bsb_ref[hd]] * (SGU_HEAD_DIM // LANES), axis=1)
        c0 = hd * SGU_HEAD_DIM
        for r0 in range(0, tile, SGU_CHUNK):
            mixed = _dot(w, vn_scr[r0:r0 + SGU_CHUNK, c0:c0 + SGU_HEAD_DIM]) + bias
            gated_scr[r0:r0 + SGU_CHUNK, c0:c0 + SGU_HEAD_DIM] = (
                _gelu_bf16(u_scr[r0:r0 + SGU_CHUNK, c0:c0 + SGU_HEAD_DIM], gk_ref) * mixed.astype(BF16))
        o_ref[...] += _dot_cols(gated_scr[:, c0:c0 + SGU_HEAD_DIM], wout_ref, c0, c0 + SGU_HEAD_DIM)


def _sgu(x, w, wb, layer, tile):
    rows = x.shape[0]
    j = layer // 2
    row_spec = pl.BlockSpec((tile, D_MODEL), lambda i: (i, 0))
    in_specs = [row_spec, _layer_spec(layer, (1, D_MODEL)), _const_spec((2 * D_SGU // COL_CHUNK, D_MODEL, COL_CHUNK)),
                _layer_spec(j, (1, 2 * D_SGU)), _layer_spec(j, (1, D_SGU)), _layer_spec(j, (1, D_SGU)),
                _layer_spec(j, (SGU_HEADS, SGU_CHUNK, SGU_CHUNK)), _layer_spec(j, (SGU_HEADS, SGU_CHUNK, LANES)),
                _const_spec((N_COL_CHUNKS, D_SGU, COL_CHUNK)), _layer_spec(j, (1, D_MODEL)),
                _const_spec((2, SGU_HEAD_DIM))]
    bsb = jnp.broadcast_to(w["sgu_b_s"][:, :, :, None], w["sgu_b_s"].shape + (LANES,))
    return pl.pallas_call(
        _sgu_kernel,
        out_shape=jax.ShapeDtypeStruct((rows, D_MODEL), F32),
        grid=(rows // tile,),
        in_specs=in_specs,
        out_specs=row_spec,
        scratch_shapes=[pltpu.VMEM((tile, D_SGU), BF16)] * 4,
        compiler_params=_params(),
        name="sgu",
    )(x, w["norm_mix_g"], wb["w_in"], w["sgu_b_in"], w["sgu_ln_g"], w["sgu_ln_b"], w["sgu_w_s"], bsb,
      wb["w_out"], w["sgu_b_out"], _gelu_consts())


def _sgu_sample_kernel(x_ref, g_ref, win_ref, bin_ref, lng_ref, lnb_ref, ws_ref, bsb_ref, wout_ref, bout_ref,
                       o_ref, v_ref, wino_ref, wouto_ref,
                       h_scr, u_scr, v_scr, s1_scr, s2_scr, gated_scr, acc_scr, *, chunk_len):
    s = pl.program_id(0)
    rows = x_ref.shape[0]
    nh = SGU_HEADS

    @pl.when(s == 0)
    def _():
        h_scr[...] = _rms(x_ref[...], g_ref[...]).astype(BF16)
        s1_scr[...] = jnp.zeros(s1_scr.shape, F32)
        s2_scr[...] = jnp.zeros(s2_scr.shape, F32)

    @pl.when(s < 2 * nh)
    def _():
        wb = win_ref[...].astype(BF16)
        _store_cols(wino_ref, wb)
        z = _dot(h_scr[...], wb) + bin_ref[...]

        @pl.when(s < nh)
        def _():
            v = _gelu(z)
            s1_scr[...] += _lane_partial(v)
            s2_scr[...] += _lane_partial(v * v)
            v_scr[s] = v

        @pl.when(s >= nh)
        def _():
            u_scr[s - nh] = z

    @pl.when(s == 2 * nh)
    def _():
        mean, rstd = _layer_norm_stats(s1_scr[...], s2_scr[...], D_SGU)
        for hd in range(nh):
            c0 = hd * SGU_HEAD_DIM
            vn = (v_scr[hd] - mean) * rstd * lng_ref[:, c0:c0 + SGU_HEAD_DIM] + lnb_ref[:, c0:c0 + SGU_HEAD_DIM]
            v_ref[:, c0:c0 + SGU_HEAD_DIM] = vn
            vnb = vn.astype(BF16)
            w = _causal(ws_ref[hd], chunk_len)
            bias = jnp.concatenate([bsb_ref[hd]] * (SGU_HEAD_DIM // LANES), axis=1)
            for r0 in range(0, rows, chunk_len):
                mixed = _dot(w, vnb[r0:r0 + chunk_len]) + bias
                gated_scr[hd, r0:r0 + chunk_len, :] = (_gelu(u_scr[hd, r0:r0 + chunk_len, :]) * mixed).astype(BF16)
        acc_scr[...] = x_ref[...] + bout_ref[...]

    @pl.when(s >= 2 * nh)
    def _():
        wb = wout_ref[...].astype(BF16)
        _store_cols(wouto_ref, wb)
        acc_scr[...] += _dot(gated_scr[s - 2 * nh], wb)

    @pl.when(s == 3 * nh - 1)
    def _():
        o_ref[...] = acc_scr[...]


def _sgu_sample(x, w, layer, chunk_len):
    rows = x.shape[0]
    j = layer // 2
    nh = SGU_HEADS
    hw = SGU_HEAD_DIM
    in_col = lambda s: (jnp.minimum(s, 2 * nh - 1) + nh) % (2 * nh)
    out_row = lambda s: jnp.maximum(s - 2 * nh, 0)
    ws = w["sgu_w_s"][:, :, :chunk_len, :chunk_len]
    bsb = jnp.broadcast_to(w["sgu_b_s"][:, :, :chunk_len, None], ws.shape[:3] + (LANES,))
    in_specs = [_const_spec((rows, D_MODEL)), _layer_spec(layer, (1, D_MODEL)),
                pl.BlockSpec((None, D_MODEL, hw), lambda s: (j, 0, in_col(s)), pipeline_mode=_STREAM),
                pl.BlockSpec((None, 1, hw), lambda s: (j, 0, in_col(s))),
                _layer_spec(j, (1, D_SGU)), _layer_spec(j, (1, D_SGU)),
                _layer_spec(j, (nh, chunk_len, chunk_len)), _layer_spec(j, (nh, chunk_len, LANES)),
                pl.BlockSpec((None, hw, D_MODEL), lambda s: (j, out_row(s), 0), pipeline_mode=_STREAM),
                _layer_spec(j, (1, D_MODEL))]
    y, v, w_in, w_out = pl.pallas_call(
        functools.partial(_sgu_sample_kernel, chunk_len=chunk_len),
        out_shape=(jax.ShapeDtypeStruct((rows, D_MODEL), F32), jax.ShapeDtypeStruct((rows, D_SGU), F32),
                   jax.ShapeDtypeStruct((2 * D_SGU // COL_CHUNK, D_MODEL, COL_CHUNK), BF16),
                   jax.ShapeDtypeStruct((N_COL_CHUNKS, D_SGU, COL_CHUNK), BF16)),
        grid=(3 * nh,),
        in_specs=in_specs,
        out_specs=(_const_spec((rows, D_MODEL)), _const_spec((rows, D_SGU)),
                   pl.BlockSpec((hw // COL_CHUNK, D_MODEL, COL_CHUNK), lambda s: (in_col(s), 0, 0)),
                   pl.BlockSpec((N_COL_CHUNKS, hw, COL_CHUNK), lambda s: (0, out_row(s), 0))),
        scratch_shapes=[pltpu.VMEM((rows, D_MODEL), BF16), pltpu.VMEM((nh, rows, hw), F32),
                        pltpu.VMEM((nh, rows, hw), F32), pltpu.VMEM((rows, LANES), F32),
                        pltpu.VMEM((rows, LANES), F32), pltpu.VMEM((nh, rows, hw), BF16),
                        pltpu.VMEM((rows, D_MODEL), F32)],
        compiler_params=_params(),
        name="sgu_sample",
    )(x, w["norm_mix_g"], w["sgu_w_in"], w["sgu_b_in"], w["sgu_ln_g"], w["sgu_ln_b"], ws, bsb,
      w["sgu_w_out"], w["sgu_b_out"])
    return y, v, dict(w_in=w_in, w_out=w_out)


def _depthwise(pad_scr, conv_scr, wdw_ref, bdw_ref, s, l, r0, out_row, row_block):
    acc = jnp.broadcast_to(bdw_ref[l], (row_block, LANES))
    for k in range(CONV_WIDTH):
        acc = acc + wdw_ref[l, k:k + 1, :] * pad_scr[s, l, r0 + k + HALO_PAD:r0 + k + HALO_PAD + row_block, :]
    conv_scr[l, out_row:out_row + row_block, :] = acc


def _conv_norm_act(conv_scr, lng_ref, lnb_ref):
    c = jnp.concatenate([conv_scr[l] for l in range(N_LANE_TILES)], axis=1)
    mu = jnp.mean(c, axis=-1, keepdims=True)
    cc = c - mu
    var = jnp.mean(cc * cc, axis=-1, keepdims=True)
    cn = cc * lax.rsqrt(var + LN_EPS) * lng_ref[...] + lnb_ref[...]
    return jax.nn.silu(cn).astype(BF16)


def _conv_state(pad_scr, s, seg_len):
    return jnp.concatenate(
        [pad_scr[s, l, HALO + seg_len - CONV_STATE:HALO + seg_len, :] for l in range(N_LANE_TILES)], axis=1)


def _conv_mixer(x, g_ref, w1a_ref, w1b_ref, b1_ref, wdw_ref, bdw_ref, lng_ref, lnb_ref, w2_ref, b2_ref, st_ref,
                pad_scr, conv_scr, *, between):
    seg_len = x.shape[0]
    h = _rms(x, g_ref[...]).astype(BF16)
    for j in range(0, D_MODEL, COL_CHUNK):
        a1 = _dot(h, w1a_ref[j // COL_CHUNK]) + b1_ref[:, j:j + COL_CHUNK]
        a2 = _dot(h, w1b_ref[j // COL_CHUNK]) + b1_ref[:, D_MODEL + j:D_MODEL + j + COL_CHUNK]
        glu = a1 * jax.nn.sigmoid(a2)
        for jj in range(0, COL_CHUNK, LANES):
            pad_scr[0, (j + jj) // LANES, HALO:HALO + seg_len, :] = glu[:, jj:jj + LANES]
        for l in range(j // LANES, (j + COL_CHUNK) // LANES):
            for r0 in range(0, seg_len, CONV_ROW_BLOCK):
                _depthwise(pad_scr, conv_scr, wdw_ref, bdw_ref, 0, l, r0, r0, CONV_ROW_BLOCK)
                between(l, r0)

    st_ref[0] = _conv_state(pad_scr, 0, seg_len)
    pad_scr[:, :, 0:HALO, :] = pad_scr[:, :, seg_len:seg_len + HALO, :]
    act = _conv_norm_act(conv_scr, lng_ref, lnb_ref)
    return x + _dot_cols(act, w2_ref) + b2_ref[...]


def _conv_sample_kernel(x_ref, g_ref, w1a_ref, w1b_ref, b1a_ref, b1b_ref, past_ref, wdw_ref, bdw_ref, lng_ref, lnb_ref,
                        w2_ref, b2_ref, o_ref, st_ref, w1ao_ref, w1bo_ref, w2o_ref,
                        h_scr, pad_scr, conv_scr, act_scr, acc_scr, *, nseg, seg_len):
    s = pl.program_id(0)
    n = D_MODEL // COL_CHUNK

    @pl.when(s == 0)
    def _():
        h_scr[...] = _rms(x_ref[...], g_ref[...]).astype(BF16)
        for l in range(N_LANE_TILES):
            pad_scr[:, l, HALO_PAD:HALO, :] = past_ref[:, :, l * LANES:(l + 1) * LANES]

    @pl.when(s < n)
    def _():
        wa = w1a_ref[...].astype(BF16)
        wb = w1b_ref[...].astype(BF16)
        w1ao_ref[0] = wa
        w1bo_ref[0] = wb
        h = h_scr[...]
        glu = (_dot(h, wa) + b1a_ref[...]) * jax.nn.sigmoid(_dot(h, wb) + b1b_ref[...])
        for jj in range(COL_CHUNK // LANES):
            l = s * (COL_CHUNK // LANES) + jj
            for q in range(nseg):
                pad_scr[q, l, HALO:HALO + seg_len, :] = glu[q * seg_len:(q + 1) * seg_len, jj * LANES:(jj + 1) * LANES]
            for q in range(nseg):
                _depthwise(pad_scr, conv_scr, wdw_ref, bdw_ref, q, l, 0, q * seg_len, seg_len)

    @pl.when(s == n)
    def _():
        for q in range(nseg):
            st_ref[q] = _conv_state(pad_scr, q, seg_len)
        act = _conv_norm_act(conv_scr, lng_ref, lnb_ref)
        for r in range(n):
            act_scr[r] = act[:, r * COL_CHUNK:(r + 1) * COL_CHUNK]
        acc_scr[...] = x_ref[...] + b2_ref[...]

    @pl.when(s >= n)
    def _():
        wb = w2_ref[...].astype(BF16)
        _store_cols(w2o_ref, wb)
        acc_scr[...] += _dot(act_scr[s - n], wb)

    @pl.when(s == 2 * n - 1)
    def _():
        o_ref[...] = acc_scr[...]


def _conv_sample(x, state, w, layer):
    rows = x.shape[0]
    nseg = state.shape[1]
    seg_len = rows // nseg
    j = layer // 2
    st_shape = state.shape[1:]
    n = D_MODEL // COL_CHUNK
    col = lambda s: jnp.minimum(s, n - 1)
    krow = lambda s: jnp.maximum(s - n, 0)
    in_specs = [_const_spec((rows, D_MODEL)), _layer_spec(layer, (1, D_MODEL)),
                pl.BlockSpec((None, D_MODEL, COL_CHUNK), lambda s: (j, 0, col(s)), pipeline_mode=_STREAM),
                pl.BlockSpec((None, D_MODEL, COL_CHUNK), lambda s: (j, 0, n + col(s)), pipeline_mode=_STREAM),
                pl.BlockSpec((None, 1, COL_CHUNK), lambda s: (j, 0, col(s))),
                pl.BlockSpec((None, 1, COL_CHUNK), lambda s: (j, 0, n + col(s))),
                _layer_spec(j, st_shape),
                _layer_spec(j, (N_LANE_TILES, CONV_WIDTH, LANES)), _layer_spec(j, (N_LANE_TILES, 1, LANES)),
                _layer_spec(j, (1, D_MODEL)), _layer_spec(j, (1, D_MODEL)),
                pl.BlockSpec((None, COL_CHUNK, D_MODEL), lambda s: (j, krow(s), 0), pipeline_mode=_STREAM),
                _layer_spec(j, (1, D_MODEL))]
    y, st, w1a, w1b, w2 = pl.pallas_call(
        functools.partial(_conv_sample_kernel, nseg=nseg, seg_len=seg_len),
        out_shape=(jax.ShapeDtypeStruct((rows, D_MODEL), F32), jax.ShapeDtypeStruct(st_shape, F32),
                   jax.ShapeDtypeStruct(_SQUARE_CHUNKED, BF16), jax.ShapeDtypeStruct(_SQUARE_CHUNKED, BF16),
                   jax.ShapeDtypeStruct(_SQUARE_CHUNKED, BF16)),
        grid=(2 * n,),
        in_specs=in_specs,
        out_specs=(_const_spec((rows, D_MODEL)), _const_spec(st_shape),
                   pl.BlockSpec((1, D_MODEL, COL_CHUNK), lambda s: (col(s), 0, 0)),
                   pl.BlockSpec((1, D_MODEL, COL_CHUNK), lambda s: (col(s), 0, 0)),
                   pl.BlockSpec((N_COL_CHUNKS, COL_CHUNK, COL_CHUNK), lambda s: (0, krow(s), 0))),
        scratch_shapes=[pltpu.VMEM((rows, D_MODEL), BF16),
                        pltpu.VMEM((nseg, N_LANE_TILES, HALO + seg_len, LANES), F32),
                        pltpu.VMEM((N_LANE_TILES, rows, LANES), F32),
                        pltpu.VMEM((n, rows, COL_CHUNK), BF16), pltpu.VMEM((rows, D_MODEL), F32)],
        compiler_params=_params(),
        name="conv_sample",
    )(x, w["norm_mix_g"], w["conv_w_pw1"], w["conv_w_pw1"], w["conv_b_pw1"], w["conv_b_pw1"], state,
      w["conv_w_dw"], w["conv_b_dw"], w["conv_ln_g"], w["conv_ln_b"], w["conv_w_pw2"], w["conv_b_pw2"])
    return y, st, dict(w1a=w1a, w1b=w1b, w2=w2)


_FFN_SLOTS = ((0, 128), (0, 384), (1, 128), (2, 128), (2, 384), (3, 128), (4, 128), (4, 384), (5, 128),
              (6, 128), (7, 128))
assert len(_FFN_SLOTS) == D_FF // COL_CHUNK


def _conv_ffn_kernel(*refs, n_tiles, tiles_per_seq, final):
    x_ref = refs[0]
    conv_params, (gf_ref, wg_ref, wu_ref, wd_ref) = refs[1:11], refs[11:15]
    rest = refs[15:]
    if final:
        gfin_ref, rest = rest[0], rest[1:]
    o_ref, st_ref, pad_scr, conv_scr, xmid_scr, xprev_scr, act_scr = rest
    i = pl.program_id(0)

    @pl.when(i == 0)
    def _():
        xmid_scr[...] = jnp.zeros(xmid_scr.shape, F32)

    xprev_scr[...] = xmid_scr[...]

    @pl.when(jnp.minimum(i, n_tiles - 1) % tiles_per_seq == 0)
    def _():
        pad_scr[:, :, 0:HALO, :] = jnp.zeros((1, N_LANE_TILES, HALO, LANES), F32)

    xp = xprev_scr[...]
    hf = _rms(xp, gf_ref[...]).astype(BF16)

    def between(l, r0):
        if (l, r0) in _FFN_SLOTS:
            _ffn_chunk(hf, wg_ref, wu_ref, act_scr, _FFN_SLOTS.index((l, r0)) * COL_CHUNK, narrow=True)

    xmid_scr[...] = _conv_mixer(x_ref[...], *conv_params, st_ref, pad_scr, conv_scr, between=between)
    y = xp + _dot_cols(act_scr[...], wd_ref)
    if final:
        y = _rms(y, gfin_ref[...])
    o_ref[...] = y


def _conv_ffn(x, w, wb_conv, wb_ffn, layer, tile, seq_len):
    rows = x.shape[0]
    n_tiles = rows // tile
    tiles_per_seq = seq_len // tile
    final = layer == DEPTH - 1
    j = layer // 2
    sq = _SQUARE_CHUNKED
    conv_specs = [_layer_spec(layer, (1, D_MODEL)), _const_spec(sq), _const_spec(sq), _layer_spec(j, (1, 2 * D_MODEL)),
                  _layer_spec(j, (N_LANE_TILES, CONV_WIDTH, LANES)), _layer_spec(j, (N_LANE_TILES, 1, LANES)),
                  _layer_spec(j, (1, D_MODEL)), _layer_spec(j, (1, D_MODEL)), _const_spec(sq),
                  _layer_spec(j, (1, D_MODEL))]
    conv_args = [w["norm_mix_g"], wb_conv["w1a"], wb_conv["w1b"], w["conv_b_pw1"], w["conv_w_dw"], w["conv_b_dw"],
                 w["conv_ln_g"], w["conv_ln_b"], wb_conv["w2"], w["conv_b_pw2"]]
    return pl.pallas_call(
        functools.partial(_conv_ffn_kernel, n_tiles=n_tiles, tiles_per_seq=tiles_per_seq, final=final),
        out_shape=(jax.ShapeDtypeStruct((rows, D_MODEL), F32),
                   jax.ShapeDtypeStruct((rows // seq_len, CONV_STATE, D_MODEL), F32)),
        grid=(n_tiles + 1,),
        in_specs=[pl.BlockSpec((tile, D_MODEL), lambda i: (jnp.minimum(i, n_tiles - 1), 0))]
        + conv_specs + _ffn_specs(layer),
        out_specs=(pl.BlockSpec((tile, D_MODEL), lambda i: (jnp.maximum(i - 1, 0), 0)),
                   pl.BlockSpec((1, CONV_STATE, D_MODEL),
                                lambda i: (jnp.minimum(i, n_tiles - 1) // tiles_per_seq, 0, 0))),
        scratch_shapes=[pltpu.VMEM((1, N_LANE_TILES, HALO + tile, LANES), F32),
                        pltpu.VMEM((N_LANE_TILES, tile, LANES), F32),
                        pltpu.VMEM((tile, D_MODEL), F32), pltpu.VMEM((tile, D_MODEL), F32),
                        pltpu.VMEM((tile, D_FF), BF16)],
        compiler_params=_params(),
        name="conv_ffn_final" if final else "conv_ffn",
    )(x, *conv_args, *_ffn_args(w, wb_ffn, layer))


def _lane_tile_major(a):
    return a.reshape(a.shape[0], a.shape[1], N_LANE_TILES, LANES).transpose(0, 2, 1, 3)


def _trunk(x_prompt, x_sample, state_conv, w):
    nb_p, t_p, _ = x_prompt.shape
    nb_s, t_s, _ = x_sample.shape
    xp = x_prompt.reshape(nb_p * t_p, D_MODEL)
    xs = x_sample.reshape(nb_s * t_s, D_MODEL)
    conv_p, conv_s, sgu_vs = [], [], []
    for layer in range(DEPTH):
        if layer % 2 == 0:
            xs, v, wb_mix = _sgu_sample(xs, w, layer, t_s)
            sgu_vs.append(v.reshape(nb_s, t_s, D_SGU))
            xs, wb_ffn = _ffn_sample(xs, w, layer)
            xp = _sgu(xp, w, wb_mix, layer, PROMPT_TILE)
            xp = _ffn(xp, w, wb_ffn, layer, PROMPT_TILE_FFN)
        else:
            xs, st_s, wb_mix = _conv_sample(xs, state_conv, w, layer)
            xs, wb_ffn = _ffn_sample(xs, w, layer)
            xp, st_p = _conv_ffn(xp, w, wb_mix, wb_ffn, layer, PROMPT_TILE, t_p)
            conv_p.append(st_p)
            conv_s.append(st_s)
    return (xp.reshape(nb_p, t_p, D_MODEL), xs.reshape(nb_s, t_s, D_MODEL), jnp.stack(conv_p), jnp.stack(conv_s),
            jnp.stack(sgu_vs))


def kernel(x_prompt, x_sample, state_conv, norm_mix_g, norm_ffn_g, norm_final_g, sgu_w_in, sgu_b_in, sgu_ln_g, sgu_ln_b, sgu_w_s, sgu_b_s, sgu_w_out, sgu_b_out, conv_w_pw1, conv_b_pw1, conv_w_dw, conv_b_dw, conv_ln_g, conv_ln_b, conv_w_pw2, conv_b_pw2, ffn_w_gate, ffn_w_up, ffn_w_down):
    assert CONV_STATE <= x_sample.shape[1] <= SGU_CHUNK and x_prompt.shape[1] % PROMPT_TILE == 0
    assert (x_prompt.shape[0] * x_prompt.shape[1]) % PROMPT_TILE_FFN == 0
    row = lambda a: a.reshape(a.shape[0], 1, a.shape[1])
    w = dict(norm_mix_g=row(norm_mix_g), norm_ffn_g=row(norm_ffn_g), norm_final_g=norm_final_g.reshape(1, D_MODEL),
             sgu_w_in=sgu_w_in, sgu_b_in=row(sgu_b_in), sgu_ln_g=row(sgu_ln_g), sgu_ln_b=row(sgu_ln_b),
             sgu_w_s=sgu_w_s, sgu_b_s=sgu_b_s, sgu_w_out=sgu_w_out, sgu_b_out=row(sgu_b_out),
             conv_w_pw1=conv_w_pw1, conv_b_pw1=row(conv_b_pw1), conv_w_dw=_lane_tile_major(conv_w_dw),
             conv_b_dw=_lane_tile_major(row(conv_b_dw)), conv_ln_g=row(conv_ln_g), conv_ln_b=row(conv_ln_b),
             conv_w_pw2=conv_w_pw2, conv_b_pw2=row(conv_b_pw2),
             ffn_w_gate=ffn_w_gate, ffn_w_up=ffn_w_up, ffn_w_down=ffn_w_down)
    return _trunk(x_prompt, x_sample, state_conv, w)
```

```python
import functools
import math

import jax
import jax.numpy as jnp
from jax import lax
from jax.experimental import pallas as pl
from jax.experimental.pallas import tpu as pltpu

D_MODEL = 1024
DEPTH = 4
SGU_CHUNK = 128
SGU_HEADS = 4
D_SGU = 3 * D_MODEL
SGU_HEAD_DIM = D_SGU // SGU_HEADS
CONV_WIDTH = 31
CONV_STATE = CONV_WIDTH - 1
D_FF = 2816
RMS_EPS = 1e-6
LN_EPS = 1e-5

LANES = 128
N_LANE_TILES = D_MODEL // LANES
HALO = 32
HALO_PAD = HALO - CONV_STATE
COL_CHUNK = 256
N_COL_CHUNKS = D_MODEL // COL_CHUNK
_SQUARE_CHUNKED = (N_COL_CHUNKS, D_MODEL, COL_CHUNK)
CONV_ROW_BLOCK = 64
VMEM_LIMIT_BYTES = 56 * 1024 * 1024

PROMPT_TILE = 512
PROMPT_TILE_FFN = 1024

F32 = jnp.float32
BF16 = jnp.bfloat16

_GELU_C = math.sqrt(2.0 / math.pi)


def _rms(x, g):
    return x * lax.rsqrt(jnp.mean(x * x, axis=-1, keepdims=True) + RMS_EPS) * g


def _gelu(z):
    w = (z * z) * (_GELU_C * 0.044715) + _GELU_C
    hz = 0.5 * z
    return hz + hz * jnp.tanh(z * w)


def _gelu_consts():
    return jnp.broadcast_to(jnp.array([[_GELU_C * 0.044715], [_GELU_C]], F32), (2, SGU_HEAD_DIM))


def _gelu_bf16(z, gk_ref):
    n = z.shape[1]
    zb = z.astype(BF16)
    w = (zb * zb) * gk_ref[0:1, 0:n].astype(BF16) + gk_ref[1:2, 0:n].astype(BF16)
    hz = zb * jnp.asarray(0.5, BF16)
    return hz + hz * jnp.tanh(zb * w)


def _dot(a, b):
    return jnp.dot(a, b, preferred_element_type=F32)


def _dot_cols(a, w_ref, k0=0, k1=None):
    return jnp.concatenate([_dot(a, w_ref[c, k0:k1, :]) for c in range(w_ref.shape[0])], axis=1)


def _store_cols(o_ref, wb):
    for c in range(o_ref.shape[0]):
        o_ref[c] = wb[:, c * COL_CHUNK:(c + 1) * COL_CHUNK]


def _lane_partial(v):
    acc = v[:, :LANES]
    for k in range(LANES, v.shape[1], LANES):
        acc = acc + v[:, k:k + LANES]
    return acc


def _layer_norm_stats(s1, s2, n):
    mean = jnp.sum(s1, axis=-1, keepdims=True) * (1.0 / n)
    var = jnp.sum(s2, axis=-1, keepdims=True) * (1.0 / n) - mean * mean
    return mean, lax.rsqrt(var + LN_EPS)


def _causal(ws, chunk_len):
    row = lax.broadcasted_iota(jnp.int32, (chunk_len, chunk_len), 0)
    col = lax.broadcasted_iota(jnp.int32, (chunk_len, chunk_len), 1)
    return jnp.where(row >= col, ws, 0.0).astype(BF16)


def _layer_spec(layer, shape):
    nd = len(shape)
    return pl.BlockSpec((None,) + tuple(shape), lambda i: (layer,) + (0,) * nd, pipeline_mode=pl.Buffered(1))


def _const_spec(shape):
    nd = len(shape)
    return pl.BlockSpec(shape, lambda i: (0,) * nd, pipeline_mode=pl.Buffered(1))


_STREAM = None


def _params():
    return pltpu.CompilerParams(dimension_semantics=("arbitrary",), vmem_limit_bytes=VMEM_LIMIT_BYTES)


def _ffn_chunk(h, wg_ref, wu_ref, act_ref, j, narrow=False):
    g = _dot(h, wg_ref[:, j:j + COL_CHUNK])
    u = _dot(h, wu_ref[:, j:j + COL_CHUNK])
    if narrow:
        gb = g.astype(BF16)
        act_ref[:, j:j + COL_CHUNK] = gb * jax.nn.sigmoid(gb) * u.astype(BF16)
    else:
        act_ref[:, j:j + COL_CHUNK] = (jax.nn.silu(g) * u).astype(BF16)


def _ffn_kernel(*refs, final):
    if final:
        x_ref, g_ref, wg_ref, wu_ref, wd_ref, gf_ref, o_ref, act_ref = refs
    else:
        x_ref, g_ref, wg_ref, wu_ref, wd_ref, o_ref, act_ref = refs
    x = x_ref[...]
    h = _rms(x, g_ref[...]).astype(BF16)
    for j in range(0, D_FF, COL_CHUNK):
        _ffn_chunk(h, wg_ref, wu_ref, act_ref, j)
    xo = x + _dot_cols(act_ref[...], wd_ref)
    if final:
        xo = _rms(xo, gf_ref[...])
    o_ref[...] = xo


def _ffn_specs(layer):
    specs = [_layer_spec(layer, (1, D_MODEL)), _const_spec((D_MODEL, D_FF)), _const_spec((D_MODEL, D_FF)),
             _const_spec((N_COL_CHUNKS, D_FF, COL_CHUNK))]
    if layer == DEPTH - 1:
        specs.append(_const_spec((1, D_MODEL)))
    return specs


def _ffn_args(w, wb, layer):
    args = [w["norm_ffn_g"], wb["gate"], wb["up"], wb["down"]]
    if layer == DEPTH - 1:
        args.append(w["norm_final_g"])
    return args


def _ffn(x, w, wb, layer, tile):
    rows = x.shape[0]
    final = layer == DEPTH - 1
    row_spec = pl.BlockSpec((tile, D_MODEL), lambda i: (i, 0))
    return pl.pallas_call(
        functools.partial(_ffn_kernel, final=final),
        out_shape=jax.ShapeDtypeStruct((rows, D_MODEL), F32),
        grid=(rows // tile,),
        in_specs=[row_spec] + _ffn_specs(layer),
        out_specs=row_spec,
        scratch_shapes=[pltpu.VMEM((tile, D_FF), BF16)],
        compiler_params=_params(),
        name="ffn_final" if final else "ffn",
    )(x, *_ffn_args(w, wb, layer))


def _ffn_sample_kernel(*refs, n_chunks, final):
    x_ref, g_ref, wg_ref, wu_ref, wd_ref = refs[:5]
    gf_ref = refs[5] if final else None
    o_ref, wgo_ref, wuo_ref, wdo_ref, h_scr, acc_scr = refs[-6:]
    j = pl.program_id(0)

    @pl.when(j == 0)
    def _():
        x = x_ref[...]
        h_scr[...] = _rms(x, g_ref[...]).astype(BF16)
        acc_scr[...] = x

    wg = wg_ref[...].astype(BF16)
    wu = wu_ref[...].astype(BF16)
    wd = wd_ref[...].astype(BF16)
    wgo_ref[...] = wg
    wuo_ref[...] = wu
    _store_cols(wdo_ref, wd)
    h = h_scr[...]
    act = (jax.nn.silu(_dot(h, wg)) * _dot(h, wu)).astype(BF16)
    acc_scr[...] += _dot(act, wd)

    @pl.when(j == n_chunks - 1)
    def _():
        y = acc_scr[...]
        if final:
            y = _rms(y, gf_ref[...])
        o_ref[...] = y


def _ffn_sample(x, w, layer):
    rows = x.shape[0]
    final = layer == DEPTH - 1
    n_chunks = D_FF // COL_CHUNK
    col_in = pl.BlockSpec((None, D_MODEL, COL_CHUNK), lambda j: (layer, 0, j), pipeline_mode=_STREAM)
    row_in = pl.BlockSpec((None, COL_CHUNK, D_MODEL), lambda j: (layer, j, 0), pipeline_mode=_STREAM)
    col_out = pl.BlockSpec((D_MODEL, COL_CHUNK), lambda j: (0, j))
    row_out = pl.BlockSpec((N_COL_CHUNKS, COL_CHUNK, COL_CHUNK), lambda j: (0, j, 0))
    in_specs = [_const_spec((rows, D_MODEL)), _layer_spec(layer, (1, D_MODEL)), col_in, col_in, row_in]
    args = [x, w["norm_ffn_g"], w["ffn_w_gate"], w["ffn_w_up"], w["ffn_w_down"]]
    if final:
        in_specs.append(_const_spec((1, D_MODEL)))
        args.append(w["norm_final_g"])
    y, gate, up, down = pl.pallas_call(
        functools.partial(_ffn_sample_kernel, n_chunks=n_chunks, final=final),
        out_shape=(jax.ShapeDtypeStruct((rows, D_MODEL), F32), jax.ShapeDtypeStruct((D_MODEL, D_FF), BF16),
                   jax.ShapeDtypeStruct((D_MODEL, D_FF), BF16), jax.ShapeDtypeStruct((N_COL_CHUNKS, D_FF, COL_CHUNK), BF16)),
        grid=(n_chunks,),
        in_specs=in_specs,
        out_specs=(_const_spec((rows, D_MODEL)), col_out, col_out, row_out),
        scratch_shapes=[pltpu.VMEM((rows, D_MODEL), BF16), pltpu.VMEM((rows, D_MODEL), F32)],
        compiler_params=_params(),
        name="ffn_sample_final" if final else "ffn_sample",
    )(*args)
    return y, dict(gate=gate, up=up, down=down)


def _sgu_kernel(x_ref, g_ref, win_ref, bin_ref, lng_ref, lnb_ref, ws_ref, bsb_ref, wout_ref, bout_ref, gk_ref,
                o_ref, u_scr, v_scr, vn_scr, gated_scr):
    tile = x_ref.shape[0]
    h = _rms(x_ref[...], g_ref[...]).astype(BF16)

    s1 = jnp.zeros((tile, LANES), F32)
    s2 = jnp.zeros((tile, LANES), F32)
    for j in range(0, D_SGU, COL_CHUNK):
        z = _dot(h, win_ref[(D_SGU + j) // COL_CHUNK]) + bin_ref[:, D_SGU + j:D_SGU + j + COL_CHUNK]
        vb = _gelu_bf16(z, gk_ref)
        v = vb.astype(F32)
        s1 = s1 + _lane_partial(v)
        s2 = s2 + _lane_partial(v * v)
        v_scr[:, j:j + COL_CHUNK] = vb
    mean, rstd = _layer_norm_stats(s1, s2, D_SGU)
    for j in range(0, D_SGU, COL_CHUNK):
        u_scr[:, j:j + COL_CHUNK] = (_dot(h, win_ref[j // COL_CHUNK]) + bin_ref[:, j:j + COL_CHUNK]).astype(BF16)
        vn = (v_scr[:, j:j + COL_CHUNK] - mean) * rstd * lng_ref[:, j:j + COL_CHUNK] + lnb_ref[:, j:j + COL_CHUNK]
        vn_scr[:, j:j + COL_CHUNK] = vn.astype(BF16)

    o_ref[...] = x_ref[...] + bout_ref[...]
    for hd in range(SGU_HEADS):
        w = _causal(ws_ref[hd], SGU_CHUNK)
        bias = jnp.concatenate([bsb_ref[hd]] * (SGU_HEAD_DIM // LANES), axis=1)
        c0 = hd * SGU_HEAD_DIM
        for r0 in range(0, tile, SGU_CHUNK):
            mixed = _dot(w, vn_scr[r0:r0 + SGU_CHUNK, c0:c0 + SGU_HEAD_DIM]) + bias
            gated_scr[r0:r0 + SGU_CHUNK, c0:c0 + SGU_HEAD_DIM] = (
                _gelu_bf16(u_scr[r0:r0 + SGU_CHUNK, c0:c0 + SGU_HEAD_DIM], gk_ref) * mixed.astype(BF16))
        o_ref[...] += _dot_cols(gated_scr[:, c0:c0 + SGU_HEAD_DIM], wout_ref, c0, c0 + SGU_HEAD_DIM)


def _sgu(x, w, wb, layer, tile):
    rows = x.shape[0]
    j = layer // 2
    row_spec = pl.BlockSpec((tile, D_MODEL), lambda i: (i, 0))
    in_specs = [row_spec, _layer_spec(layer, (1, D_MODEL)), _const_spec((2 * D_SGU // COL_CHUNK, D_MODEL, COL_CHUNK)),
                _layer_spec(j, (1, 2 * D_SGU)), _layer_spec(j, (1, D_SGU)), _layer_spec(j, (1, D_SGU)),
                _layer_spec(j, (SGU_HEADS, SGU_CHUNK, SGU_CHUNK)), _layer_spec(j, (SGU_HEADS, SGU_CHUNK, LANES)),
                _const_spec((N_COL_CHUNKS, D_SGU, COL_CHUNK)), _layer_spec(j, (1, D_MODEL)),
                _const_spec((2, SGU_HEAD_DIM))]
    bsb = jnp.broadcast_to(w["sgu_b_s"][:, :, :, None], w["sgu_b_s"].shape + (LANES,))
    return pl.pallas_call(
        _sgu_kernel,
        out_shape=jax.ShapeDtypeStruct((rows, D_MODEL), F32),
        grid=(rows // tile,),
        in_specs=in_specs,
        out_specs=row_spec,
        scratch_shapes=[pltpu.VMEM((tile, D_SGU), BF16)] * 4,
        compiler_params=_params(),
        name="sgu",
    )(x, w["norm_mix_g"], wb["w_in"], w["sgu_b_in"], w["sgu_ln_g"], w["sgu_ln_b"], w["sgu_w_s"], bsb,
      wb["w_out"], w["sgu_b_out"], _gelu_consts())


def _sgu_sample_kernel(x_ref, g_ref, win_ref, bin_ref, lng_ref, lnb_ref, ws_ref, bsb_ref, wout_ref, bout_ref,
                       o_ref, v_ref, wino_ref, wouto_ref,
                       h_scr, u_scr, v_scr, s1_scr, s2_scr, gated_scr, acc_scr, *, chunk_len):
    s = pl.program_id(0)
    rows = x_ref.shape[0]
    nh = SGU_HEADS

    @pl.when(s == 0)
    def _():
        h_scr[...] = _rms(x_ref[...], g_ref[...]).astype(BF16)
        s1_scr[...] = jnp.zeros(s1_scr.shape, F32)
        s2_scr[...] = jnp.zeros(s2_scr.shape, F32)

    @pl.when(s < 2 * nh)
    def _():
        wb = win_ref[...].astype(BF16)
        _store_cols(wino_ref, wb)
        z = _dot(h_scr[...], wb) + bin_ref[...]

        @pl.when(s < nh)
        def _():
            v = _gelu(z)
            s1_scr[...] += _lane_partial(v)
            s2_scr[...] += _lane_partial(v * v)
            v_scr[s] = v

        @pl.when(s >= nh)
        def _():
            u_scr[s - nh] = z

    @pl.when(s == 2 * nh)
    def _():
        mean, rstd = _layer_norm_stats(s1_scr[...], s2_scr[...], D_SGU)
        for hd in range(nh):
            c0 = hd * SGU_HEAD_DIM
            vn = (v_scr[hd] - mean) * rstd * lng_ref[:, c0:c0 + SGU_HEAD_DIM] + lnb_ref[:, c0:c0 + SGU_HEAD_DIM]
            v_ref[:, c0:c0 + SGU_HEAD_DIM] = vn
            vnb = vn.astype(BF16)
            w = _causal(ws_ref[hd], chunk_len)
            bias = jnp.concatenate([bsb_ref[hd]] * (SGU_HEAD_DIM // LANES), axis=1)
            for r0 in range(0, rows, chunk_len):
                mixed = _dot(w, vnb[r0:r0 + chunk_len]) + bias
                gated_scr[hd, r0:r0 + chunk_len, :] = (_gelu(u_scr[hd, r0:r0 + chunk_len, :]) * mixed).astype(BF16)
        acc_scr[...] = x_ref[...] + bout_ref[...]

    @pl.when(s >= 2 * nh)
    def _():
        wb = wout_ref[...].astype(BF16)
        _store_cols(wouto_ref, wb)
        acc_scr[...] += _dot(gated_scr[s - 2 * nh], wb)

    @pl.when(s == 3 * nh - 1)
    def _():
        o_ref[...] = acc_scr[...]


def _sgu_sample(x, w, layer, chunk_len):
    rows = x.shape[0]
    j = layer // 2
    nh = SGU_HEADS
    hw = SGU_HEAD_DIM
    in_col = lambda s: (jnp.minimum(s, 2 * nh - 1) + nh) % (2 * nh)
    out_row = lambda s: jnp.maximum(s - 2 * nh, 0)
    ws = w["sgu_w_s"][:, :, :chunk_len, :chunk_len]
    bsb = jnp.broadcast_to(w["sgu_b_s"][:, :, :chunk_len, None], ws.shape[:3] + (LANES,))
    in_specs = [_const_spec((rows, D_MODEL)), _layer_spec(layer, (1, D_MODEL)),
                pl.BlockSpec((None, D_MODEL, hw), lambda s: (j, 0, in_col(s)), pipeline_mode=_STREAM),
                pl.BlockSpec((None, 1, hw), lambda s: (j, 0, in_col(s))),
                _layer_spec(j, (1, D_SGU)), _layer_spec(j, (1, D_SGU)),
                _layer_spec(j, (nh, chunk_len, chunk_len)), _layer_spec(j, (nh, chunk_len, LANES)),
                pl.BlockSpec((None, hw, D_MODEL), lambda s: (j, out_row(s), 0), pipeline_mode=_STREAM),
                _layer_spec(j, (1, D_MODEL))]
    y, v, w_in, w_out = pl.pallas_call(
        functools.partial(_sgu_sample_kernel, chunk_len=chunk_len),
        out_shape=(jax.ShapeDtypeStruct((rows, D_MODEL), F32), jax.ShapeDtypeStruct((rows, D_SGU), F32),
                   jax.ShapeDtypeStruct((2 * D_SGU // COL_CHUNK, D_MODEL, COL_CHUNK), BF16),
                   jax.ShapeDtypeStruct((N_COL_CHUNKS, D_SGU, COL_CHUNK), BF16)),
        grid=(3 * nh,),
        in_specs=in_specs,
        out_specs=(_const_spec((rows, D_MODEL)), _const_spec((rows, D_SGU)),
                   pl.BlockSpec((hw // COL_CHUNK, D_MODEL, COL_CHUNK), lambda s: (in_col(s), 0, 0)),
                   pl.BlockSpec((N_COL_CHUNKS, hw, COL_CHUNK), lambda s: (0, out_row(s), 0))),
        scratch_shapes=[pltpu.VMEM((rows, D_MODEL), BF16), pltpu.VMEM((nh, rows, hw), F32),
                        pltpu.VMEM((nh, rows, hw), F32), pltpu.VMEM((rows, LANES), F32),
                        pltpu.VMEM((rows, LANES), F32), pltpu.VMEM((nh, rows, hw), BF16),
                        pltpu.VMEM((rows, D_MODEL), F32)],
        compiler_params=_params(),
        name="sgu_sample",
    )(x, w["norm_mix_g"], w["sgu_w_in"], w["sgu_b_in"], w["sgu_ln_g"], w["sgu_ln_b"], ws, bsb,
      w["sgu_w_out"], w["sgu_b_out"])
    return y, v, dict(w_in=w_in, w_out=w_out)


def _depthwise(pad_scr, conv_scr, wdw_ref, bdw_ref, s, l, r0, out_row, row_block):
    acc = jnp.broadcast_to(bdw_ref[l], (row_block, LANES))
    for k in range(CONV_WIDTH):
        acc = acc + wdw_ref[l, k:k + 1, :] * pad_scr[s, l, r0 + k + HALO_PAD:r0 + k + HALO_PAD + row_block, :]
    conv_scr[l, out_row:out_row + row_block, :] = acc


def _conv_norm_act(conv_scr, lng_ref, lnb_ref):
    c = jnp.concatenate([conv_scr[l] for l in range(N_LANE_TILES)], axis=1)
    mu = jnp.mean(c, axis=-1, keepdims=True)
    cc = c - mu
    var = jnp.mean(cc * cc, axis=-1, keepdims=True)
    cn = cc * lax.rsqrt(var + LN_EPS) * lng_ref[...] + lnb_ref[...]
    return jax.nn.silu(cn).astype(BF16)


def _conv_state(pad_scr, s, seg_len):
    return jnp.concatenate(
        [pad_scr[s, l, HALO + seg_len - CONV_STATE:HALO + seg_len, :] for l in range(N_LANE_TILES)], axis=1)


def _conv_mixer(x, g_ref, w1a_ref, w1b_ref, b1_ref, wdw_ref, bdw_ref, lng_ref, lnb_ref, w2_ref, b2_ref, st_ref,
                pad_scr, conv_scr, *, between):
    seg_len = x.shape[0]
    h = _rms(x, g_ref[...]).astype(BF16)
    for j in range(0, D_MODEL, COL_CHUNK):
        a1 = _dot(h, w1a_ref[j // COL_CHUNK]) + b1_ref[:, j:j + COL_CHUNK]
        a2 = _dot(h, w1b_ref[j // COL_CHUNK]) + b1_ref[:, D_MODEL + j:D_MODEL + j + COL_CHUNK]
        glu = a1 * jax.nn.sigmoid(a2)
        for jj in range(0, COL_CHUNK, LANES):
            pad_scr[0, (j + jj) // LANES, HALO:HALO + seg_len, :] = glu[:, jj:jj + LANES]
        for l in range(j // LANES, (j + COL_CHUNK) // LANES):
            for r0 in range(0, seg_len, CONV_ROW_BLOCK):
                _depthwise(pad_scr, conv_scr, wdw_ref, bdw_ref, 0, l, r0, r0, CONV_ROW_BLOCK)
                between(l, r0)

    st_ref[0] = _conv_state(pad_scr, 0, seg_len)
    pad_scr[:, :, 0:HALO, :] = pad_scr[:, :, seg_len:seg_len + HALO, :]
    act = _conv_norm_act(conv_scr, lng_ref, lnb_ref)
    return x + _dot_cols(act, w2_ref) + b2_ref[...]


def _conv_sample_kernel(x_ref, g_ref, w1a_ref, w1b_ref, b1a_ref, b1b_ref, past_ref, wdw_ref, bdw_ref, lng_ref, lnb_ref,
                        w2_ref, b2_ref, o_ref, st_ref, w1ao_ref, w1bo_ref, w2o_ref,
                        h_scr, pad_scr, conv_scr, act_scr, acc_scr, *, nseg, seg_len):
    s = pl.program_id(0)
    n = D_MODEL // COL_CHUNK

    @pl.when(s == 0)
    def _():
        h_scr[...] = _rms(x_ref[...], g_ref[...]).astype(BF16)
        for l in range(N_LANE_TILES):
            pad_scr[:, l, HALO_PAD:HALO, :] = past_ref[:, :, l * LANES:(l + 1) * LANES]

    @pl.when(s < n)
    def _():
        wa = w1a_ref[...].astype(BF16)
        wb = w1b_ref[...].astype(BF16)
        w1ao_ref[0] = wa
        w1bo_ref[0] = wb
        h = h_scr[...]
        glu = (_dot(h, wa) + b1a_ref[...]) * jax.nn.sigmoid(_dot(h, wb) + b1b_ref[...])
        for jj in range(COL_CHUNK // LANES):
            l = s * (COL_CHUNK // LANES) + jj
            for q in range(nseg):
                pad_scr[q, l, HALO:HALO + seg_len, :] = glu[q * seg_len:(q + 1) * seg_len, jj * LANES:(jj + 1) * LANES]
            for q in range(nseg):
                _depthwise(pad_scr, conv_scr, wdw_ref, bdw_ref, q, l, 0, q * seg_len, seg_len)

    @pl.when(s == n)
    def _():
        for q in range(nseg):
            st_ref[q] = _conv_state(pad_scr, q, seg_len)
        act = _conv_norm_act(conv_scr, lng_ref, lnb_ref)
        for r in range(n):
            act_scr[r] = act[:, r * COL_CHUNK:(r + 1) * COL_CHUNK]
        acc_scr[...] = x_ref[...] + b2_ref[...]

    @pl.when(s >= n)
    def _():
        wb = w2_ref[...].astype(BF16)
        _store_cols(w2o_ref, wb)
        acc_scr[...] += _dot(act_scr[s - n], wb)

    @pl.when(s == 2 * n - 1)
    def _():
        o_ref[...] = acc_scr[...]


def _conv_sample(x, state, w, layer):
    rows = x.shape[0]
    nseg = state.shape[1]
    seg_len = rows // nseg
    j = layer // 2
    st_shape = state.shape[1:]
    n = D_MODEL // COL_CHUNK
    col = lambda s: jnp.minimum(s, n - 1)
    krow = lambda s: jnp.maximum(s - n, 0)
    in_specs = [_const_spec((rows, D_MODEL)), _layer_spec(layer, (1, D_MODEL)),
                pl.BlockSpec((None, D_MODEL, COL_CHUNK), lambda s: (j, 0, col(s)), pipeline_mode=_STREAM),
                pl.BlockSpec((None, D_MODEL, COL_CHUNK), lambda s: (j, 0, n + col(s)), pipeline_mode=_STREAM),
                pl.BlockSpec((None, 1, COL_CHUNK), lambda s: (j, 0, col(s))),
                pl.BlockSpec((None, 1, COL_CHUNK), lambda s: (j, 0, n + col(s))),
                _layer_spec(j, st_shape),
                _layer_spec(j, (N_LANE_TILES, CONV_WIDTH, LANES)), _layer_spec(j, (N_LANE_TILES, 1, LANES)),
                _layer_spec(j, (1, D_MODEL)), _layer_spec(j, (1, D_MODEL)),
                pl.BlockSpec((None, COL_CHUNK, D_MODEL), lambda s: (j, krow(s), 0), pipeline_mode=_STREAM),
                _layer_spec(j, (1, D_MODEL))]
    y, st, w1a, w1b, w2 = pl.pallas_call(
        functools.partial(_conv_sample_kernel, nseg=nseg, seg_len=seg_len),
        out_shape=(jax.ShapeDtypeStruct((rows, D_MODEL), F32), jax.ShapeDtypeStruct(st_shape, F32),
                   jax.ShapeDtypeStruct(_SQUARE_CHUNKED, BF16), jax.ShapeDtypeStruct(_SQUARE_CHUNKED, BF16),
                   jax.ShapeDtypeStruct(_SQUARE_CHUNKED, BF16)),
        grid=(2 * n,),
        in_specs=in_specs,
        out_specs=(_const_spec((rows, D_MODEL)), _const_spec(st_shape),
                   pl.BlockSpec((1, D_MODEL, COL_CHUNK), lambda s: (col(s), 0, 0)),
                   pl.BlockSpec((1, D_MODEL, COL_CHUNK), lambda s: (col(s), 0, 0)),
                   pl.BlockSpec((N_COL_CHUNKS, COL_CHUNK, COL_CHUNK), lambda s: (0, krow(s), 0))),
        scratch_shapes=[pltpu.VMEM((rows, D_MODEL), BF16),
                        pltpu.VMEM((nseg, N_LANE_TILES, HALO + seg_len, LANES), F32),
                        pltpu.VMEM((N_LANE_TILES, rows, LANES), F32),
                        pltpu.VMEM((n, rows, COL_CHUNK), BF16), pltpu.VMEM((rows, D_MODEL), F32)],
        compiler_params=_params(),
        name="conv_sample",
    )(x, w["norm_mix_g"], w["conv_w_pw1"], w["conv_w_pw1"], w["conv_b_pw1"], w["conv_b_pw1"], state,
      w["conv_w_dw"], w["conv_b_dw"], w["conv_ln_g"], w["conv_ln_b"], w["conv_w_pw2"], w["conv_b_pw2"])
    return y, st, dict(w1a=w1a, w1b=w1b, w2=w2)


_FFN_SLOTS = ((0, 128), (0, 384), (1, 128), (2, 128), (2, 384), (3, 128), (4, 128), (4, 384), (5, 128),
              (6, 128), (7, 128))
assert len(_FFN_SLOTS) == D_FF // COL_CHUNK


def _conv_ffn_kernel(*refs, n_tiles, tiles_per_seq, final):
    x_ref = refs[0]
    conv_params, (gf_ref, wg_ref, wu_ref, wd_ref) = refs[1:11], refs[11:15]
    rest = refs[15:]
    if final:
        gfin_ref, rest = rest[0], rest[1:]
    o_ref, st_ref, pad_scr, conv_scr, xmid_scr, act_scr = rest
    i = pl.program_id(0)
    slot = i % 2

    @pl.when(i == 0)
    def _():
        xmid_scr[...] = jnp.zeros(xmid_scr.shape, F32)

    @pl.when(jnp.minimum(i, n_tiles - 1) % tiles_per_seq == 0)
    def _():
        pad_scr[:, :, 0:HALO, :] = jnp.zeros((1, N_LANE_TILES, HALO, LANES), F32)

    xp = xmid_scr[1 - slot]
    hf = _rms(xp, gf_ref[...]).astype(BF16)

    def between(l, r0):
        if (l, r0) in _FFN_SLOTS:
            _ffn_chunk(hf, wg_ref, wu_ref, act_scr, _FFN_SLOTS.index((l, r0)) * COL_CHUNK)

    xmid_scr[slot] = _conv_mixer(x_ref[...], *conv_params, st_ref, pad_scr, conv_scr, between=between)
    y = xp + _dot_cols(act_scr[...], wd_ref)
    if final:
        y = _rms(y, gfin_ref[...])
    o_ref[...] = y


def _conv_ffn(x, w, wb_conv, wb_ffn, layer, tile, seq_len):
    rows = x.shape[0]
    n_tiles = rows // tile
    tiles_per_seq = seq_len // tile
    final = layer == DEPTH - 1
    j = layer // 2
    sq = _SQUARE_CHUNKED
    conv_specs = [_layer_spec(layer, (1, D_MODEL)), _const_spec(sq), _const_spec(sq), _layer_spec(j, (1, 2 * D_MODEL)),
                  _layer_spec(j, (N_LANE_TILES, CONV_WIDTH, LANES)), _layer_spec(j, (N_LANE_TILES, 1, LANES)),
                  _layer_spec(j, (1, D_MODEL)), _layer_spec(j, (1, D_MODEL)), _const_spec(sq),
                  _layer_spec(j, (1, D_MODEL))]
    conv_args = [w["norm_mix_g"], wb_conv["w1a"], wb_conv["w1b"], w["conv_b_pw1"], w["conv_w_dw"], w["conv_b_dw"],
                 w["conv_ln_g"], w["conv_ln_b"], wb_conv["w2"], w["conv_b_pw2"]]
    return pl.pallas_call(
        functools.partial(_conv_ffn_kernel, n_tiles=n_tiles, tiles_per_seq=tiles_per_seq, final=final),
        out_shape=(jax.ShapeDtypeStruct((rows, D_MODEL), F32),
                   jax.ShapeDtypeStruct((rows // seq_len, CONV_STATE, D_MODEL), F32)),
        grid=(n_tiles + 1,),
        in_specs=[pl.BlockSpec((tile, D_MODEL), lambda i: (jnp.minimum(i, n_tiles - 1), 0))]
        + conv_specs + _ffn_specs(layer),
        out_specs=(pl.BlockSpec((tile, D_MODEL), lambda i: (jnp.maximum(i - 1, 0), 0)),
                   pl.BlockSpec((1, CONV_STATE, D_MODEL),
                                lambda i: (jnp.minimum(i, n_tiles - 1) // tiles_per_seq, 0, 0))),
        scratch_shapes=[pltpu.VMEM((1, N_LANE_TILES, HALO + tile, LANES), F32),
                        pltpu.VMEM((N_LANE_TILES, tile, LANES), F32),
                        pltpu.VMEM((2, tile, D_MODEL), F32),
                        pltpu.VMEM((tile, D_FF), BF16)],
        compiler_params=_params(),
        name="conv_ffn_final" if final else "conv_ffn",
    )(x, *conv_args, *_ffn_args(w, wb_ffn, layer))


def _lane_tile_major(a):
    return a.reshape(a.shape[0], a.shape[1], N_LANE_TILES, LANES).transpose(0, 2, 1, 3)


def _trunk(x_prompt, x_sample, state_conv, w):
    nb_p, t_p, _ = x_prompt.shape
    nb_s, t_s, _ = x_sample.shape
    xp = x_prompt.reshape(nb_p * t_p, D_MODEL)
    xs = x_sample.reshape(nb_s * t_s, D_MODEL)
    conv_p, conv_s, sgu_vs = [], [], []
    for layer in range(DEPTH):
        if layer % 2 == 0:
            xs, v, wb_mix = _sgu_sample(xs, w, layer, t_s)
            sgu_vs.append(v.reshape(nb_s, t_s, D_SGU))
            xs, wb_ffn = _ffn_sample(xs, w, layer)
            xp = _sgu(xp, w, wb_mix, layer, PROMPT_TILE)
            xp = _ffn(xp, w, wb_ffn, layer, PROMPT_TILE_FFN)
        else:
            xs, st_s, wb_mix = _conv_sample(xs, state_conv, w, layer)
            xs, wb_ffn = _ffn_sample(xs, w, layer)
            xp, st_p = _conv_ffn(xp, w, wb_mix, wb_ffn, layer, PROMPT_TILE, t_p)
            conv_p.append(st_p)
            conv_s.append(st_s)
    return (xp.reshape(nb_p, t_p, D_MODEL), xs.reshape(nb_s, t_s, D_MODEL), jnp.stack(conv_p), jnp.stack(conv_s),
            jnp.stack(sgu_vs))


def kernel(x_prompt, x_sample, state_conv, norm_mix_g, norm_ffn_g, norm_final_g, sgu_w_in, sgu_b_in, sgu_ln_g, sgu_ln_b, sgu_w_s, sgu_b_s, sgu_w_out, sgu_b_out, conv_w_pw1, conv_b_pw1, conv_w_dw, conv_b_dw, conv_ln_g, conv_ln_b, conv_w_pw2, conv_b_pw2, ffn_w_gate, ffn_w_up, ffn_w_down):
    assert CONV_STATE <= x_sample.shape[1] <= SGU_CHUNK and x_prompt.shape[1] % PROMPT_TILE == 0
    assert (x_prompt.shape[0] * x_prompt.shape[1]) % PROMPT_TILE_FFN == 0
    row = lambda a: a.reshape(a.shape[0], 1, a.shape[1])
    w = dict(norm_mix_g=row(norm_mix_g), norm_ffn_g=row(norm_ffn_g), norm_final_g=norm_final_g.reshape(1, D_MODEL),
             sgu_w_in=sgu_w_in, sgu_b_in=row(sgu_b_in), sgu_ln_g=row(sgu_ln_g), sgu_ln_b=row(sgu_ln_b),
             sgu_w_s=sgu_w_s, sgu_b_s=sgu_b_s, sgu_w_out=sgu_w_out, sgu_b_out=row(sgu_b_out),
             conv_w_pw1=conv_w_pw1, conv_b_pw1=row(conv_b_pw1), conv_w_dw=_lane_tile_major(conv_w_dw),
             conv_b_dw=_lane_tile_major(row(conv_b_dw)), conv_ln_g=row(conv_ln_g), conv_ln_b=row(conv_ln_b),
             conv_w_pw2=conv_w_pw2, conv_b_pw2=row(conv_b_pw2),
             ffn_w_gate=ffn_w_gate, ffn_w_up=ffn_w_up, ffn_w_down=ffn_w_down)
    return _trunk(x_prompt, x_sample, state_conv, w)
```
